```python
import jax, jax.numpy as jnp
from jax import lax
import numpy as np

D_MODEL = 1024
BATCH = 8
SEQ = 4096
DEPTH = 2

HEAD_DIM = 64
CONV_WIDTH = D_MODEL // 4
RET_WIDTH = D_MODEL // 4
ATT_WIDTH = D_MODEL // 2
MIX_WIDTH = CONV_WIDTH + RET_WIDTH + ATT_WIDTH
RET_HEADS = RET_WIDTH // HEAD_DIM
ATT_HEADS = ATT_WIDTH // HEAD_DIM
CONV_K = 31
RET_CHUNK = 128
MOBA_BLOCK = 256
MOBA_TOPK = 3
Q_CHUNK = 32
RMS_EPS = 1e-6
LN_EPS = 1e-5
NEG = -1e30
IN_SIZES = (CONV_WIDTH, CONV_WIDTH, CONV_WIDTH,
            RET_WIDTH, RET_WIDTH, RET_WIDTH, RET_WIDTH,
            ATT_WIDTH, ATT_WIDTH, ATT_WIDTH, ATT_WIDTH)
IN_WIDTH = sum(IN_SIZES)
SPLIT_POINTS = tuple(int(s) for s in np.cumsum(IN_SIZES)[:-1])

kernel_name = "hymba_conv_retention_moba_hybrid"


def rmsnorm(x, g):
    xf = x.astype(jnp.float32)
    y = xf * lax.rsqrt(jnp.mean(xf * xf, axis=-1, keepdims=True) + RMS_EPS)
    return (y * g.astype(jnp.float32)).astype(x.dtype)


def split_heads(t, n_heads):
    b, s, _ = t.shape
    return t.reshape(b, s, n_heads, -1).transpose(0, 2, 1, 3)


def merge_heads(t):
    b, n, s, d = t.shape
    return t.transpose(0, 2, 1, 3).reshape(b, s, n * d)


def conformer_conv(val, glu_gate, conv_w, conv_b, ln_g, ln_b, pw_w, pw_b):
    u = val * jax.nn.sigmoid(glu_gate)
    y = lax.conv_general_dilated(u, conv_w[:, None, :], window_strides=(1,),
                                 padding=[(CONV_K - 1, 0)],
                                 dimension_numbers=('NWC', 'WIO', 'NWC'),
                                 feature_group_count=CONV_WIDTH) + conv_b
    yf = y.astype(jnp.float32)
    mu = jnp.mean(yf, axis=-1, keepdims=True)
    var = jnp.mean(jnp.square(yf - mu), axis=-1, keepdims=True)
    yn = ((yf - mu) * lax.rsqrt(var + LN_EPS) * ln_g + ln_b).astype(val.dtype)
    return jax.nn.silu(yn) @ pw_w + pw_b


def retention(q, k, v):
    out_dtype = q.dtype
    q, k, v = (t.astype(jnp.float32) for t in (q, k, v))
    b, h, s, d = q.shape
    c = RET_CHUNK
    n = s // c
    k = k * (d ** -0.5)
    log_g = jnp.log(1.0 - jnp.exp2(-5.0 - jnp.arange(h, dtype=jnp.float32)))
    i = jnp.arange(c, dtype=jnp.float32)
    diff = i[:, None] - i[None, :]
    dec = jnp.where(diff >= 0, jnp.exp(log_g[:, None, None] * jnp.maximum(diff, 0.0)), 0.0)
    qc = q.reshape(b, h, n, c, d)
    kc = k.reshape(b, h, n, c, d)
    vc = v.reshape(b, h, n, c, d)
    scores = jnp.einsum('bhnid,bhnjd->bhnij', qc, kc) * dec[None, :, None]
    intra = jnp.einsum('bhnij,bhnje->bhnie', scores, vc)
    zeta = jnp.exp(log_g[:, None] * (c - 1 - i))
    kv = jnp.einsum('bhnjd,bhnje->nbhde', kc * zeta[None, :, None, :, None], vc)
    g_chunk = jnp.exp(log_g * c)[None, :, None, None]

    def step(state, kv_n):
        return kv_n + g_chunk * state, state

    _, r_prev = lax.scan(step, jnp.zeros((b, h, d, d), jnp.float32), kv)
    xi = jnp.exp(log_g[:, None] * (i + 1.0))
    cross = jnp.einsum('bhnid,nbhde->bhnie', qc, r_prev) * xi[None, :, None, :, None]
    o = (intra + cross).reshape(b, h, s, d)
    mu = jnp.mean(o, axis=-1, keepdims=True)
    var = jnp.mean(jnp.square(o - mu), axis=-1, keepdims=True)
    return ((o - mu) * lax.rsqrt(var + LN_EPS)).astype(out_dtype)


def moba_attention(q, k, v):
    out_dtype = q.dtype
    q, k, v = (t.astype(jnp.float32) for t in (q, k, v))
    b, h, s, d = q.shape
    blk = MOBA_BLOCK
    nblk = -(-s // blk)
    sp = nblk * blk
    kpad = jnp.pad(k, ((0, 0), (0, 0), (0, sp - s), (0, 0)))
    vpad = jnp.pad(v, ((0, 0), (0, 0), (0, sp - s), (0, 0)))
    kb = kpad.reshape(b, h, nblk, blk, d)
    vb = vpad.reshape(b, h, nblk, blk, d)
    scale = d ** -0.5
    slopes = jnp.exp2(-8.0 * (jnp.arange(h, dtype=jnp.float32) + 1.0) / h)
    k_mean = jnp.mean(kb, axis=3)
    gate = jnp.einsum('bhsd,bhnd->bhsn', q, k_mean)
    q_blk = jnp.arange(s) // blk
    past = jnp.arange(nblk)[None, :] < q_blk[:, None]
    gate = jnp.where(past[None, None], gate, NEG)
    topk = min(MOBA_TOPK, nblk)
    _, sel = lax.top_k(gate, topk)
    bi = jnp.arange(b)[:, None, None, None]
    hi = jnp.arange(h)[None, :, None, None]
    n_chunks = s // Q_CHUNK

    def body(ci):
        t0 = ci * Q_CHUNK
        qc = lax.dynamic_slice_in_dim(q, t0, Q_CHUNK, axis=2)
        ic = lax.dynamic_slice_in_dim(sel, t0, Q_CHUNK, axis=2)
        t = (t0 + jnp.arange(Q_CHUNK)).astype(jnp.float32)
        j = t0 // blk
        k_sel = kb[bi, hi, ic]
        v_sel = vb[bi, hi, ic]
        s_sel = jnp.einsum('bhcd,bhckld->bhckl', qc, k_sel) * scale
        pos_sel = (ic[..., None] * blk + jnp.arange(blk)).astype(jnp.float32)
        bias_sel = -slopes[None, :, None, None, None] * (t[None, None, :, None, None] - pos_sel)
        valid = (ic < j)[..., None]
        l_sel = jnp.where(valid, s_sel + bias_sel, NEG).reshape(b, h, Q_CHUNK, topk * blk)
        k_own = lax.dynamic_slice_in_dim(kpad, j * blk, blk, axis=2)
        v_own = lax.dynamic_slice_in_dim(vpad, j * blk, blk, axis=2)
        s_own = jnp.einsum('bhcd,bhld->bhcl', qc, k_own) * scale
        pos_own = (j * blk + jnp.arange(blk)).astype(jnp.float32)
        dist = t[:, None] - pos_own[None, :]
        l_own = jnp.where((dist >= 0)[None, None],
                          s_own - slopes[None, :, None, None] * dist[None, None], NEG)
        p = jax.nn.softmax(jnp.concatenate([l_sel, l_own], axis=-1), axis=-1)
        p_sel = p[..., :topk * blk].reshape(b, h, Q_CHUNK, topk, blk)
        p_own = p[..., topk * blk:]
        return (jnp.einsum('bhckl,bhckld->bhcd', p_sel, v_sel)
                + jnp.einsum('bhcl,bhld->bhcd', p_own, v_own))

    out = lax.map(body, jnp.arange(n_chunks))
    out = out.transpose(1, 2, 0, 3, 4).reshape(b, h, s, d)
    return out.astype(out_dtype)


def setup_inputs(seed: int = 0) -> dict:
    key = jax.random.key(seed)
    ks = jax.random.split(key, 12)
    f32 = jnp.float32
    x = jax.random.normal(ks[0], (BATCH, SEQ, D_MODEL), f32)
    norm_g = 1.0 + 0.05 * jax.random.normal(ks[1], (DEPTH, D_MODEL), f32)
    w_in = jax.random.normal(ks[2], (DEPTH, D_MODEL, IN_WIDTH), f32) * D_MODEL ** -0.5
    conv_w = jax.random.normal(ks[3], (DEPTH, CONV_K, CONV_WIDTH), f32) * CONV_K ** -0.5
    conv_b = 0.02 * jax.random.normal(ks[4], (DEPTH, CONV_WIDTH), f32)
    conv_ln_g = 1.0 + 0.05 * jax.random.normal(ks[5], (DEPTH, CONV_WIDTH), f32)
    conv_ln_b = 0.02 * jax.random.normal(ks[6], (DEPTH, CONV_WIDTH), f32)
    conv_pw_w = jax.random.normal(ks[7], (DEPTH, CONV_WIDTH, CONV_WIDTH), f32) * CONV_WIDTH ** -0.5
    conv_pw_b = 0.02 * jax.random.normal(ks[8], (DEPTH, CONV_WIDTH), f32)
    w_out = jax.random.normal(ks[9], (DEPTH, MIX_WIDTH, D_MODEL), f32) * MIX_WIDTH ** -0.5
    final_g = 1.0 + 0.05 * jax.random.normal(ks[10], (D_MODEL,), f32)
    return {"x": x, "norm_g": norm_g, "w_in": w_in, "conv_w": conv_w, "conv_b": conv_b,
            "conv_ln_g": conv_ln_g, "conv_ln_b": conv_ln_b, "conv_pw_w": conv_pw_w,
            "conv_pw_b": conv_pw_b, "w_out": w_out, "final_g": final_g}


def reference(x, norm_g, w_in, conv_w, conv_b, conv_ln_g, conv_ln_b, conv_pw_w, conv_pw_b,
              w_out, final_g):
    for layer in range(DEPTH):
        hn = rmsnorm(x, norm_g[layer])
        proj = hn @ w_in[layer]
        (a_val, a_glu, a_gate, r_q, r_k, r_v, r_gate,
         m_q, m_k, m_v, m_gate) = jnp.split(proj, SPLIT_POINTS, axis=-1)
        y_a = conformer_conv(a_val, a_glu, conv_w[layer], conv_b[layer], conv_ln_g[layer],
                             conv_ln_b[layer], conv_pw_w[layer], conv_pw_b[layer])
        y_a = y_a * jax.nn.silu(a_gate)
        y_r = merge_heads(retention(split_heads(r_q, RET_HEADS), split_heads(r_k, RET_HEADS),
                                    split_heads(r_v, RET_HEADS)))
        y_r = y_r * jax.nn.silu(r_gate)
        y_m = merge_heads(moba_attention(split_heads(m_q, ATT_HEADS), split_heads(m_k, ATT_HEADS),
                                         split_heads(m_v, ATT_HEADS)))
        y_m = y_m * jax.nn.silu(m_gate)
        y = jnp.concatenate([y_a, y_r, y_m], axis=-1) @ w_out[layer]
        x = x + y
    return rmsnorm(x, final_g)
```

```python
import functools

import numpy as np
import jax
import jax.numpy as jnp
from jax import lax
from jax.experimental import pallas as pl
from jax.experimental.pallas import tpu as pltpu

D_MODEL = 1024
DEPTH = 2
HEAD_DIM = 64
CONV_WIDTH = 256
RET_WIDTH = 256
ATT_WIDTH = 512
RET_HEADS = RET_WIDTH // HEAD_DIM
ATT_HEADS = ATT_WIDTH // HEAD_DIM
CONV_K = 31
RET_CHUNK = 128
MOBA_BLOCK = 256
MOBA_TOPK = 3
RMS_EPS = 1e-6
LN_EPS = 1e-5
NEG = -1e30

LANES = 128
HALO = 32
VMEM_LIMIT = 56 * 1024 * 1024

F32 = jnp.float32
BF16 = jnp.bfloat16

_C_AVAL, _C_AGLU, _C_AGATE = 0, 256, 512
_C_RQ, _C_RK, _C_RV, _C_RGATE = 768, 1024, 1280, 1536
_C_MQ, _C_MK, _C_MGATE, _C_END = 1792, 2304, 2816, 3328


def _sigmoid(x):
    return 1.0 / (1.0 + jnp.exp(-x))


def _dot(a, b):
    return jnp.dot(a, b, preferred_element_type=F32)


def _dot_nt(a, b):
    return lax.dot_general(a, b, (((1,), (1,)), ((), ())), preferred_element_type=F32)


def _dot_tn(a, b):
    return lax.dot_general(a, b, (((0,), (0,)), ((), ())), preferred_element_type=F32)


def _inproj_kernel(x_ref, g_ref, w_ref, wvt_ref, qtab_ref,
                   u_ref, sg_ref, rq_ref, rk_ref, rv_ref, qa_ref, ka_ref, vt_ref, hn_ref, *, tm):
    x = x_ref[0]
    ms = jnp.mean(x * x, axis=-1, keepdims=True)
    hn_ref[...] = (x * lax.rsqrt(ms + RMS_EPS) * g_ref[...]).astype(BF16)

    def proj(lo, hi):
        return _dot(hn_ref[...], w_ref[:, lo:hi])

    a = proj(_C_AVAL, _C_AGATE)
    u_ref[0] = (a[:, :CONV_WIDTH] * _sigmoid(a[:, CONV_WIDTH:])).astype(BF16)

    def silu(t):
        return t * _sigmoid(t)

    sg_ref[0, :, 0:256] = silu(proj(_C_AGATE, _C_RQ)).astype(BF16)
    sg_ref[0, :, 256:512] = silu(proj(_C_RGATE, _C_MQ)).astype(BF16)
    sg_ref[0, :, 512:1024] = silu(proj(_C_MGATE, _C_END)).astype(BF16)

    r = proj(_C_RQ, _C_RGATE)
    rq_ref[0] = r[:, 0:256].astype(BF16)
    rk_ref[0] = r[:, 256:512].astype(BF16)
    rv_ref[0] = r[:, 512:768].astype(BF16)

    lane = lax.broadcasted_iota(jnp.int32, (tm, LANES), 1)
    low = lane < HEAD_DIM
    row = lax.broadcasted_iota(jnp.int32, (tm, LANES), 0) + pl.program_id(1) * tm
    pos_lo = (row % MOBA_BLOCK).astype(F32)
    pos_hi = (row - row % MOBA_BLOCK).astype(F32)
    zero = jnp.zeros((tm, LANES), F32)
    kb_lowhead = jnp.where(lane == HEAD_DIM, pos_lo, jnp.where(lane == HEAD_DIM + 1, pos_hi, zero))
    kb_highhead = jnp.where(lane == 0, pos_lo, jnp.where(lane == 1, pos_hi, zero))

    mq = proj(_C_MQ, _C_MK) * (HEAD_DIM ** -0.5)
    mk = proj(_C_MK, _C_MGATE)
    for g in range(ATT_HEADS // 2):
        qg = mq[:, g * LANES:(g + 1) * LANES]
        kg = mk[:, g * LANES:(g + 1) * LANES]
        qa_ref[0, g] = jnp.where(low, qg, qtab_ref[g:g + 1, :]).astype(BF16)
        qa_ref[0, g + 4] = jnp.where(low, qtab_ref[g + 4:g + 5, :], qg).astype(BF16)
        ka_ref[0, g] = jnp.where(low, kg, kb_lowhead).astype(BF16)
        ka_ref[0, g + 4] = jnp.where(low, kb_highhead, kg).astype(BF16)

    vt = _dot_nt(wvt_ref[...], hn_ref[...])
    sub = lax.broadcasted_iota(jnp.int32, (HEAD_DIM, MOBA_BLOCK), 0)
    ones_row = jnp.where(sub == 0, 1.0, 0.0).astype(BF16)
    for h in range(ATT_HEADS):
        for c in range(tm // MOBA_BLOCK):
            vt_ref[0, h, c, 0:HEAD_DIM, :] = vt[h * HEAD_DIM:(h + 1) * HEAD_DIM,
                                                c * MOBA_BLOCK:(c + 1) * MOBA_BLOCK].astype(BF16)
            vt_ref[0, h, c, HEAD_DIM:, :] = ones_row


def _inproj(x, g, w, wvt, qtab, *, tm):
    b, s, d = x.shape
    grid = (b, s // tm)
    row_spec = lambda c: pl.BlockSpec((1, tm, c), lambda i, j: (i, j, 0))
    full2 = lambda a: pl.BlockSpec(a.shape, lambda i, j: (0, 0))
    out_shape = (
        jax.ShapeDtypeStruct((b, s, CONV_WIDTH), BF16),
        jax.ShapeDtypeStruct((b, s, D_MODEL), BF16),
        jax.ShapeDtypeStruct((b, s, RET_WIDTH), BF16),
        jax.ShapeDtypeStruct((b, s, RET_WIDTH), BF16),
        jax.ShapeDtypeStruct((b, s, RET_WIDTH), BF16),
        jax.ShapeDtypeStruct((b, ATT_HEADS, s, LANES), BF16),
        jax.ShapeDtypeStruct((b, ATT_HEADS, s, LANES), BF16),
        jax.ShapeDtypeStruct((b, ATT_HEADS, s // MOBA_BLOCK, LANES, MOBA_BLOCK), BF16),
    )
    out_specs = (
        row_spec(CONV_WIDTH), row_spec(D_MODEL), row_spec(RET_WIDTH), row_spec(RET_WIDTH),
        row_spec(RET_WIDTH),
        pl.BlockSpec((1, ATT_HEADS, tm, LANES), lambda i, j: (i, 0, j, 0)),
        pl.BlockSpec((1, ATT_HEADS, tm, LANES), lambda i, j: (i, 0, j, 0)),
        pl.BlockSpec((1, ATT_HEADS, tm // MOBA_BLOCK, LANES, MOBA_BLOCK), lambda i, j: (i, 0, j, 0, 0)),
    )
    return pl.pallas_call(
        functools.partial(_inproj_kernel, tm=tm),
        grid=grid,
        in_specs=[row_spec(d), full2(g), full2(w), full2(wvt), full2(qtab)],
        out_specs=out_specs,
        out_shape=out_shape,
        scratch_shapes=[pltpu.VMEM((tm, d), BF16)],
        compiler_params=pltpu.CompilerParams(
            dimension_semantics=("parallel", "parallel"), vmem_limit_bytes=VMEM_LIMIT),
        name="inproj",
    )(x, g, w, wvt, qtab)


def _conv_kernel(u_ref, halo_ref, cw_ref, cb_ref, lg_ref, lb_ref, pw_ref, pb_ref, y_ref, ext_ref, *, ts):
    first = pl.program_id(1) == 0
    halo = halo_ref[0].astype(F32)
    ext_ref[0:HALO, :] = jnp.where(first, 0.0, halo)
    ext_ref[HALO:, :] = u_ref[0].astype(F32)
    acc = jnp.zeros((ts, CONV_WIDTH), F32) + cb_ref[...]
    base = HALO - (CONV_K - 1)
    for k in range(CONV_K):
        acc = acc + ext_ref[base + k:base + k + ts, :] * cw_ref[k:k + 1, :]
    mu = jnp.mean(acc, axis=-1, keepdims=True)
    dlt = acc - mu
    var = jnp.mean(dlt * dlt, axis=-1, keepdims=True)
    yn = dlt * lax.rsqrt(var + LN_EPS) * lg_ref[...] + lb_ref[...]
    sw = (yn * _sigmoid(yn)).astype(BF16)
    y_ref[0] = (_dot(sw, pw_ref[...]) + pb_ref[...]).astype(BF16)


def _conv(u, cw, cb, lg, lb, pw, pb, *, ts):
    b, s, c = u.shape
    per = ts // HALO
    full2 = lambda a: pl.BlockSpec(a.shape, lambda i, j: (0, 0))
    return pl.pallas_call(
        functools.partial(_conv_kernel, ts=ts),
        grid=(b, s // ts),
        in_specs=[
            pl.BlockSpec((1, ts, c), lambda i, j: (i, j, 0)),
            pl.BlockSpec((1, HALO, c), lambda i, j: (i, jnp.maximum(j * per - 1, 0), 0)),
            full2(cw), full2(cb), full2(lg), full2(lb), full2(pw), full2(pb),
        ],
        out_specs=pl.BlockSpec((1, ts, c), lambda i, j: (i, j, 0)),
        out_shape=jax.ShapeDtypeStruct((b, s, c), BF16),
        scratch_shapes=[pltpu.VMEM((HALO + ts, c), F32)],
        compiler_params=pltpu.CompilerParams(
            dimension_semantics=("parallel", "parallel"), vmem_limit_bytes=VMEM_LIMIT),
        name="conv",
    )(u, u, cw, cb, lg, lb, pw, pb)


def _ret_tables():
    h = jnp.arange(RET_HEADS, dtype=F32)
    log_g = jnp.log(1.0 - jnp.exp2(-5.0 - h))
    i = jnp.arange(RET_CHUNK, dtype=F32)
    diff = i[:, None] - i[None, :]
    dec = jnp.where(diff >= 0, jnp.exp(log_g[:, None, None] * jnp.maximum(diff, 0.0)), 0.0)
    scale = HEAD_DIM ** -0.5
    dec_stack = (dec * scale).reshape(RET_HEADS * RET_CHUNK, RET_CHUNK)
    zeta = jnp.exp(log_g[:, None] * (RET_CHUNK - 1 - i))
    xi = jnp.exp(log_g[:, None] * (i + 1.0))
    zeta_tab = jnp.repeat(zeta.T, HEAD_DIM, axis=1) * scale
    xi_tab = jnp.repeat(xi.T, HEAD_DIM, axis=1)
    g_chunk = jnp.exp(log_g * RET_CHUNK)
    head_of = jnp.arange(RET_WIDTH) // HEAD_DIM
    same = head_of[:, None] == head_of[None, :]
    gtab = jnp.where(same, g_chunk[head_of][:, None], 0.0).astype(F32)
    bmask = same.astype(F32)
    avg = (same.astype(F32) / HEAD_DIM).astype(BF16)
    return dec_stack, zeta_tab, xi_tab, gtab, bmask, avg


def _group_mean(z, avg):
    hi = z.astype(BF16)
    lo = (z - hi.astype(F32)).astype(BF16)
    return _dot(hi, avg) + _dot(lo, avg)


def _ret_kernel(q_ref, k_ref, v_ref, dec_ref, zeta_ref, xi_ref, gtab_ref, bmask_ref, avg_ref,
                y_ref, state_ref, *, chunks):
    @pl.when(pl.program_id(1) == 0)
    def _():
        state_ref[...] = jnp.zeros_like(state_ref)

    c = RET_CHUNK
    lane_head = lax.broadcasted_iota(jnp.int32, (c, RET_WIDTH), 1) // HEAD_DIM
    avg = avg_ref[...]
    for ci in range(chunks):
        sl = slice(ci * c, (ci + 1) * c)
        q = q_ref[0, sl, :]
        k = k_ref[0, sl, :]
        v = v_ref[0, sl, :]
        qf = q.astype(F32)
        qs = jnp.concatenate([jnp.where(lane_head == h, qf, 0.0).astype(BF16) for h in range(RET_HEADS)],
                             axis=0)
        sc = _dot_nt(qs, k) * dec_ref[...]
        full = _dot(sc.astype(BF16), v)
        intra = jnp.zeros((c, RET_WIDTH), F32)
        for h in range(RET_HEADS):
            intra = jnp.where(lane_head == h, full[h * c:(h + 1) * c, :], intra)
        st = state_ref[...]
        cross = _dot(q, st.astype(BF16)) * xi_ref[...]
        o = intra + cross
        kz = (k.astype(F32) * zeta_ref[...]).astype(BF16)
        kv = _dot_tn(kz, v)
        state_ref[...] = kv * bmask_ref[...] + st * gtab_ref[...]
        mu = _group_mean(o, avg)
        dlt = o - mu
        var = _group_mean(dlt * dlt, avg)
        y_ref[0, sl, :] = (dlt * lax.rsqrt(var + LN_EPS)).astype(BF16)


def _retention(rq, rk, rv, tables, *, ts):
    b, s, c = rq.shape
    blk = pl.BlockSpec((1, ts, c), lambda i, j: (i, j, 0))
    full2 = lambda a: pl.BlockSpec(a.shape, lambda i, j: (0, 0))
    return pl.pallas_call(
        functools.partial(_ret_kernel, chunks=ts // RET_CHUNK),
        grid=(b, s // ts),
        in_specs=[blk, blk, blk] + [full2(t) for t in tables],
        out_specs=blk,
        out_shape=jax.ShapeDtypeStruct((b, s, c), BF16),
        scratch_shapes=[pltpu.VMEM((RET_WIDTH, RET_WIDTH), F32)],
        compiler_params=pltpu.CompilerParams(
            dimension_semantics=("parallel", "arbitrary"), vmem_limit_bytes=VMEM_LIMIT),
        name="retention",
    )(rq, rk, rv, *tables)


def _moba_kernel(q_ref, k_ref, vt_ref, y_ref, km_ref, mb_ref, s_ref, *, nblk):
    hp = pl.program_id(1)
    j = pl.program_id(2)
    blk = MOBA_BLOCK
    lane = lax.broadcasted_iota(jnp.int32, (1, LANES), 1)
    real_lanes = (lane < HEAD_DIM) == (hp < 2)

    @pl.when(j == 0)
    def _():
        for h in range(2):
            rows = []
            for i in range(nblk):
                kb = k_ref[0, h, i * blk:(i + 1) * blk, :].astype(F32)
                rows.append(jnp.sum(kb, axis=0, keepdims=True) * (1.0 / blk))
            km = jnp.where(real_lanes, jnp.concatenate(rows, axis=0), 0.0)
            hi = km.astype(BF16)
            km_ref[h, 0:nblk, :] = hi
            km_ref[h, nblk:2 * nblk, :] = (km - hi.astype(F32)).astype(BF16)

    blk_id = lax.broadcasted_iota(jnp.int32, (nblk, blk), 0)
    kpos = lax.broadcasted_iota(jnp.int32, (blk, blk), 0)
    qpos = lax.broadcasted_iota(jnp.int32, (blk, blk), 1)
    outs = []
    for h in range(2):
        q = q_ref[0, h]
        g2 = _dot_nt(km_ref[h], q)
        gate = g2[0:nblk] + g2[nblk:2 * nblk]
        gm = jnp.where(blk_id < j, gate, NEG)
        cnt = jnp.zeros((nblk, blk), jnp.int32)
        for i2 in range(nblk):
            r = gm[i2:i2 + 1, :]
            beats = (r > gm) | ((r == gm) & (blk_id > i2))
            cnt = cnt + beats.astype(jnp.int32)
        sel = ((cnt < MOBA_TOPK) & (blk_id < j)) | (blk_id == j)
        mb_ref[...] = jnp.where(sel, 0.0, NEG)

        def qk(i, m):
            kb = k_ref[0, h, pl.ds(pl.multiple_of(i * blk, blk), blk), :]
            s = _dot_nt(kb, q) + mb_ref[pl.ds(i, 1), :]
            s = jnp.where((i == j) & (kpos > qpos), NEG, s)
            s_ref[i] = s
            return jnp.maximum(m, jnp.max(s, axis=0, keepdims=True))

        m = lax.fori_loop(0, j + 1, qk, jnp.full((1, blk), NEG, F32))

        def pv(i, acc):
            p = jnp.exp(s_ref[i] - m).astype(BF16)
            return acc + _dot(vt_ref[0, h, i], p)

        acc = lax.fori_loop(0, j + 1, pv, jnp.zeros((LANES, blk), F32))
        outs.append(acc[0:HEAD_DIM, :] * (1.0 / acc[HEAD_DIM:HEAD_DIM + 1, :]))
    y_ref[0] = jnp.concatenate(outs, axis=0).T.astype(BF16)


def _moba(qa, ka, vt):
    b, nh, s, _ = qa.shape
    nblk = s // MOBA_BLOCK
    return pl.pallas_call(
        functools.partial(_moba_kernel, nblk=nblk),
        grid=(b, nh // 2, nblk),
        in_specs=[
            pl.BlockSpec((1, 2, MOBA_BLOCK, LANES), lambda i, p, j: (i, p, j, 0)),
            pl.BlockSpec((1, 2, s, LANES), lambda i, p, j: (i, p, 0, 0)),
            pl.BlockSpec((1, 2, nblk, LANES, MOBA_BLOCK), lambda i, p, j: (i, p, 0, 0, 0)),
        ],
        out_specs=pl.BlockSpec((1, MOBA_BLOCK, LANES), lambda i, p, j: (i, j, p)),
        out_shape=jax.ShapeDtypeStruct((b, s, ATT_WIDTH), BF16),
        scratch_shapes=[
            pltpu.VMEM((2, 2 * nblk, LANES), BF16),
            pltpu.VMEM((nblk, MOBA_BLOCK), F32),
            pltpu.VMEM((nblk, MOBA_BLOCK, MOBA_BLOCK), F32),
        ],
        compiler_params=pltpu.CompilerParams(
            dimension_semantics=("parallel", "parallel", "arbitrary"), vmem_limit_bytes=VMEM_LIMIT),
        name="moba",
    )(qa, ka, vt)


def _outproj_kernel(ya_ref, yr_ref, ym_ref, sg_ref, x_ref, w_ref, fg_ref, o_ref, *, final):
    mix = jnp.concatenate([ya_ref[0], yr_ref[0], ym_ref[0]], axis=-1).astype(F32)
    mix = (mix * sg_ref[0].astype(F32)).astype(BF16)
    xn = x_ref[0] + _dot(mix, w_ref[...])
    if final:
        ms = jnp.mean(xn * xn, axis=-1, keepdims=True)
        xn = xn * lax.rsqrt(ms + RMS_EPS) * fg_ref[...]
    o_ref[0] = xn


def _outproj(ya, yr, ym, sg, x, w, fg, *, tm, final):
    b, s, d = x.shape
    row_spec = lambda c: pl.BlockSpec((1, tm, c), lambda i, j: (i, j, 0))
    full2 = lambda a: pl.BlockSpec(a.shape, lambda i, j: (0, 0))
    return pl.pallas_call(
        functools.partial(_outproj_kernel, final=final),
        grid=(b, s // tm),
        in_specs=[row_spec(CONV_WIDTH), row_spec(RET_WIDTH), row_spec(ATT_WIDTH), row_spec(d),
                  row_spec(d), full2(w), full2(fg)],
        out_specs=row_spec(d),
        out_shape=jax.ShapeDtypeStruct((b, s, d), F32),
        compiler_params=pltpu.CompilerParams(
            dimension_semantics=("parallel", "parallel"), vmem_limit_bytes=VMEM_LIMIT),
        name="outproj",
    )(ya, yr, ym, sg, x, w, fg)


def _pack_weights(w_in_l):
    sizes = (CONV_WIDTH,) * 3 + (RET_WIDTH,) * 4 + (ATT_WIDTH,) * 4
    pts = np.cumsum(sizes)[:-1]
    a_val, a_glu, a_gate, r_q, r_k, r_v, r_gate, m_q, m_k, m_v, m_gate = jnp.split(w_in_l, pts, axis=-1)
    perm = np.concatenate([np.r_[g * HEAD_DIM:(g + 1) * HEAD_DIM, (g + 4) * HEAD_DIM:(g + 5) * HEAD_DIM]
                           for g in range(ATT_HEADS // 2)])
    w = jnp.concatenate([a_val, a_glu, a_gate, r_q, r_k, r_v, r_gate, m_q[:, perm], m_k[:, perm], m_gate],
                        axis=-1).astype(BF16)
    return w, m_v.T.astype(BF16)


def _q_bias_table():
    slopes = np.exp2(-8.0 * (np.arange(ATT_HEADS, dtype=np.float64) + 1.0) / ATT_HEADS)
    tab = np.zeros((ATT_HEADS, LANES), np.float32)
    for h in range(ATT_HEADS):
        base = HEAD_DIM if h < ATT_HEADS // 2 else 0
        tab[h, base] = slopes[h]
        tab[h, base + 1] = slopes[h]
    return jnp.asarray(tab)


def kernel(x, norm_g, w_in, conv_w, conv_b, conv_ln_g, conv_ln_b, conv_pw_w, conv_pw_b, w_out, final_g):
    b, s, d = x.shape
    tm = min(512, s)
    qtab = _q_bias_table()
    tables = _ret_tables()
    fg = final_g.reshape(1, d)
    for layer in range(DEPTH):
        w, wvt = _pack_weights(w_in[layer])
        u, sg, rq, rk, rv, qa, ka, vt = _inproj(x, norm_g[layer].reshape(1, d), w, wvt, qtab, tm=tm)
        ya = _conv(u, conv_w[layer], conv_b[layer].reshape(1, -1), conv_ln_g[layer].reshape(1, -1),
                   conv_ln_b[layer].reshape(1, -1), conv_pw_w[layer].astype(BF16),
                   conv_pw_b[layer].reshape(1, -1), ts=tm)
        yr = _retention(rq, rk, rv, tables, ts=tm)
        ym = _moba(qa, ka, vt)
        x = _outproj(ya, yr, ym, sg, x, w_out[layer].astype(BF16), fg, tm=tm, final=(layer == DEPTH - 1))
    return x
```

```python
import functools

import numpy as np
import jax
import jax.numpy as jnp
from jax import lax
from jax.experimental import pallas as pl
from jax.experimental.pallas import tpu as pltpu

D_MODEL = 1024
DEPTH = 2
HEAD_DIM = 64
CONV_WIDTH = 256
RET_WIDTH = 256
ATT_WIDTH = 512
RET_HEADS = RET_WIDTH // HEAD_DIM
ATT_HEADS = ATT_WIDTH // HEAD_DIM
CONV_K = 31
RET_CHUNK = 128
MOBA_BLOCK = 256
MOBA_TOPK = 3
RMS_EPS = 1e-6
LN_EPS = 1e-5
NEG = -1e30

LANES = 128
HALO = 32
VMEM_LIMIT = 56 * 1024 * 1024

F32 = jnp.float32
BF16 = jnp.bfloat16

_C_AVAL, _C_AGLU, _C_AGATE = 0, 256, 512
_C_RQ, _C_RK, _C_RV, _C_RGATE = 768, 1024, 1280, 1536
_C_MQ, _C_MK, _C_MGATE, _C_END = 1792, 2304, 2816, 3328


def _sigmoid(x):
    return 1.0 / (1.0 + jnp.exp(-x))


def _dot(a, b):
    return jnp.dot(a, b, preferred_element_type=F32)


def _dot_nt(a, b):
    return lax.dot_general(a, b, (((1,), (1,)), ((), ())), preferred_element_type=F32)


def _dot_tn(a, b):
    return lax.dot_general(a, b, (((0,), (0,)), ((), ())), preferred_element_type=F32)


def _inproj_kernel(x_ref, g_ref, w_ref, wvt_ref, qtab_ref,
                   u_ref, sg_ref, rq_ref, rk_ref, rv_ref, qa_ref, ka_ref, vt_ref, hn_ref, *, tm):
    x = x_ref[0]
    ms = jnp.mean(x * x, axis=-1, keepdims=True)
    hn_ref[...] = (x * lax.rsqrt(ms + RMS_EPS) * g_ref[...]).astype(BF16)

    def proj(lo, hi):
        return _dot(hn_ref[...], w_ref[:, lo:hi])

    a = proj(_C_AVAL, _C_AGATE)
    u_ref[0] = (a[:, :CONV_WIDTH] * _sigmoid(a[:, CONV_WIDTH:])).astype(BF16)

    def silu(t):
        return t * _sigmoid(t)

    sg_ref[0, :, 0:256] = silu(proj(_C_AGATE, _C_RQ)).astype(BF16)
    sg_ref[0, :, 256:512] = silu(proj(_C_RGATE, _C_MQ)).astype(BF16)
    sg_ref[0, :, 512:1024] = silu(proj(_C_MGATE, _C_END)).astype(BF16)

    r = proj(_C_RQ, _C_RGATE)
    rq_ref[0] = r[:, 0:256].astype(BF16)
    rk_ref[0] = r[:, 256:512].astype(BF16)
    rv_ref[0] = r[:, 512:768].astype(BF16)

    lane = lax.broadcasted_iota(jnp.int32, (tm, LANES), 1)
    low = lane < HEAD_DIM
    row = lax.broadcasted_iota(jnp.int32, (tm, LANES), 0) + pl.program_id(1) * tm
    pos_lo = (row % MOBA_BLOCK).astype(F32)
    pos_hi = (row - row % MOBA_BLOCK).astype(F32)
    zero = jnp.zeros((tm, LANES), F32)
    kb_lowhead = jnp.where(lane == HEAD_DIM, pos_lo, jnp.where(lane == HEAD_DIM + 1, pos_hi, zero))
    kb_highhead = jnp.where(lane == 0, pos_lo, jnp.where(lane == 1, pos_hi, zero))

    mq = proj(_C_MQ, _C_MK) * (HEAD_DIM ** -0.5)
    mk = proj(_C_MK, _C_MGATE)
    for g in range(ATT_HEADS // 2):
        qg = mq[:, g * LANES:(g + 1) * LANES]
        kg = mk[:, g * LANES:(g + 1) * LANES]
        qa_ref[0, g] = jnp.where(low, qg, qtab_ref[g:g + 1, :]).astype(BF16)
        qa_ref[0, g + 4] = jnp.where(low, qtab_ref[g + 4:g + 5, :], qg).astype(BF16)
        ka_ref[0, g] = jnp.where(low, kg, kb_lowhead).astype(BF16)
        ka_ref[0, g + 4] = jnp.where(low, kb_highhead, kg).astype(BF16)

    vt = _dot_nt(wvt_ref[...], hn_ref[...])
    sub = lax.broadcasted_iota(jnp.int32, (HEAD_DIM, MOBA_BLOCK), 0)
    ones_row = jnp.where(sub == 0, 1.0, 0.0).astype(BF16)
    for h in range(ATT_HEADS):
        for c in range(tm // MOBA_BLOCK):
            vt_ref[0, h, c, 0:HEAD_DIM, :] = vt[h * HEAD_DIM:(h + 1) * HEAD_DIM,
                                                c * MOBA_BLOCK:(c + 1) * MOBA_BLOCK].astype(BF16)
            vt_ref[0, h, c, HEAD_DIM:, :] = ones_row


def _inproj(x, g, w, wvt, qtab, *, tm):
    b, s, d = x.shape
    grid = (b, s // tm)
    row_spec = lambda c: pl.BlockSpec((1, tm, c), lambda i, j: (i, j, 0))
    full2 = lambda a: pl.BlockSpec(a.shape, lambda i, j: (0, 0))
    out_shape = (
        jax.ShapeDtypeStruct((b, s, CONV_WIDTH), BF16),
        jax.ShapeDtypeStruct((b, s, D_MODEL), BF16),
        jax.ShapeDtypeStruct((b, s, RET_WIDTH), BF16),
        jax.ShapeDtypeStruct((b, s, RET_WIDTH), BF16),
        jax.ShapeDtypeStruct((b, s, RET_WIDTH), BF16),
        jax.ShapeDtypeStruct((b, ATT_HEADS, s, LANES), BF16),
        jax.ShapeDtypeStruct((b, ATT_HEADS, s, LANES), BF16),
        jax.ShapeDtypeStruct((b, ATT_HEADS, s // MOBA_BLOCK, LANES, MOBA_BLOCK), BF16),
    )
    out_specs = (
        row_spec(CONV_WIDTH), row_spec(D_MODEL), row_spec(RET_WIDTH), row_spec(RET_WIDTH),
        row_spec(RET_WIDTH),
        pl.BlockSpec((1, ATT_HEADS, tm, LANES), lambda i, j: (i, 0, j, 0)),
        pl.BlockSpec((1, ATT_HEADS, tm, LANES), lambda i, j: (i, 0, j, 0)),
        pl.BlockSpec((1, ATT_HEADS, tm // MOBA_BLOCK, LANES, MOBA_BLOCK), lambda i, j: (i, 0, j, 0, 0)),
    )
    return pl.pallas_call(
        functools.partial(_inproj_kernel, tm=tm),
        grid=grid,
        in_specs=[row_spec(d), full2(g), full2(w), full2(wvt), full2(qtab)],
        out_specs=out_specs,
        out_shape=out_shape,
        scratch_shapes=[pltpu.VMEM((tm, d), BF16)],
        compiler_params=pltpu.CompilerParams(
            dimension_semantics=("parallel", "parallel"), vmem_limit_bytes=VMEM_LIMIT),
        name="inproj",
    )(x, g, w, wvt, qtab)


def _conv_kernel(u_ref, halo_ref, cw_ref, cb_ref, lg_ref, lb_ref, pw_ref, pb_ref, y_ref, ext_ref, *, ts):
    first = pl.program_id(1) == 0
    halo = halo_ref[0].astype(F32)
    ext_ref[0:HALO, :] = jnp.where(first, 0.0, halo)
    ext_ref[HALO:, :] = u_ref[0].astype(F32)
    acc = jnp.zeros((ts, CONV_WIDTH), F32) + cb_ref[...]
    base = HALO - (CONV_K - 1)
    for k in range(CONV_K):
        acc = acc + ext_ref[base + k:base + k + ts, :] * cw_ref[k:k + 1, :]
    mu = jnp.mean(acc, axis=-1, keepdims=True)
    dlt = acc - mu
    var = jnp.mean(dlt * dlt, axis=-1, keepdims=True)
    yn = dlt * lax.rsqrt(var + LN_EPS) * lg_ref[...] + lb_ref[...]
    sw = (yn * _sigmoid(yn)).astype(BF16)
    y_ref[0] = (_dot(sw, pw_ref[...]) + pb_ref[...]).astype(BF16)


def _conv(u, cw, cb, lg, lb, pw, pb, *, ts):
    b, s, c = u.shape
    per = ts // HALO
    full2 = lambda a: pl.BlockSpec(a.shape, lambda i, j: (0, 0))
    return pl.pallas_call(
        functools.partial(_conv_kernel, ts=ts),
        grid=(b, s // ts),
        in_specs=[
            pl.BlockSpec((1, ts, c), lambda i, j: (i, j, 0)),
            pl.BlockSpec((1, HALO, c), lambda i, j: (i, jnp.maximum(j * per - 1, 0), 0)),
            full2(cw), full2(cb), full2(lg), full2(lb), full2(pw), full2(pb),
        ],
        out_specs=pl.BlockSpec((1, ts, c), lambda i, j: (i, j, 0)),
        out_shape=jax.ShapeDtypeStruct((b, s, c), BF16),
        scratch_shapes=[pltpu.VMEM((HALO + ts, c), F32)],
        compiler_params=pltpu.CompilerParams(
            dimension_semantics=("parallel", "parallel"), vmem_limit_bytes=VMEM_LIMIT),
        name="conv",
    )(u, u, cw, cb, lg, lb, pw, pb)


def _ret_tables():
    h = jnp.arange(RET_HEADS, dtype=F32)
    log_g = jnp.log(1.0 - jnp.exp2(-5.0 - h))
    i = jnp.arange(RET_CHUNK, dtype=F32)
    diff = i[:, None] - i[None, :]
    dec = jnp.where(diff >= 0, jnp.exp(log_g[:, None, None] * jnp.maximum(diff, 0.0)), 0.0)
    scale = HEAD_DIM ** -0.5
    dec_stack = (dec * scale).reshape(RET_HEADS * RET_CHUNK, RET_CHUNK)
    zeta = jnp.exp(log_g[:, None] * (RET_CHUNK - 1 - i))
    xi = jnp.exp(log_g[:, None] * (i + 1.0))
    zeta_tab = jnp.repeat(zeta.T, HEAD_DIM, axis=1) * scale
    xi_tab = jnp.repeat(xi.T, HEAD_DIM, axis=1)
    g_chunk = jnp.exp(log_g * RET_CHUNK)
    head_of = jnp.arange(RET_WIDTH) // HEAD_DIM
    same = head_of[:, None] == head_of[None, :]
    gtab = jnp.where(same, g_chunk[head_of][:, None], 0.0).astype(F32)
    bmask = same.astype(F32)
    avg = (same.astype(F32) / HEAD_DIM).astype(BF16)
    return dec_stack, zeta_tab, xi_tab, gtab, bmask, avg


def _group_mean(z, avg):
    hi = z.astype(BF16)
    lo = (z - hi.astype(F32)).astype(BF16)
    return _dot(hi, avg) + _dot(lo, avg)


def _ret_kernel(q_ref, k_ref, v_ref, dec_ref, zeta_ref, xi_ref, gtab_ref, bmask_ref, avg_ref,
                y_ref, state_ref, *, chunks):
    @pl.when(pl.program_id(1) == 0)
    def _():
        state_ref[...] = jnp.zeros_like(state_ref)

    c = RET_CHUNK
    lane_head = lax.broadcasted_iota(jnp.int32, (c, RET_WIDTH), 1) // HEAD_DIM
    avg = avg_ref[...]
    for ci in range(chunks):
        sl = slice(ci * c, (ci + 1) * c)
        q = q_ref[0, sl, :]
        k = k_ref[0, sl, :]
        v = v_ref[0, sl, :]
        qf = q.astype(F32)
        qs = jnp.concatenate([jnp.where(lane_head == h, qf, 0.0).astype(BF16) for h in range(RET_HEADS)],
                             axis=0)
        sc = _dot_nt(qs, k) * dec_ref[...]
        full = _dot(sc.astype(BF16), v)
        intra = jnp.zeros((c, RET_WIDTH), F32)
        for h in range(RET_HEADS):
            intra = jnp.where(lane_head == h, full[h * c:(h + 1) * c, :], intra)
        st = state_ref[...]
        cross = _dot(q, st.astype(BF16)) * xi_ref[...]
        o = intra + cross
        kz = (k.astype(F32) * zeta_ref[...]).astype(BF16)
        kv = _dot_tn(kz, v)
        state_ref[...] = kv * bmask_ref[...] + st * gtab_ref[...]
        mu = _group_mean(o, avg)
        dlt = o - mu
        var = _group_mean(dlt * dlt, avg)
        y_ref[0, sl, :] = (dlt * lax.rsqrt(var + LN_EPS)).astype(BF16)


def _retention(rq, rk, rv, tables, *, ts):
    b, s, c = rq.shape
    blk = pl.BlockSpec((1, ts, c), lambda i, j: (i, j, 0))
    full2 = lambda a: pl.BlockSpec(a.shape, lambda i, j: (0, 0))
    return pl.pallas_call(
        functools.partial(_ret_kernel, chunks=ts // RET_CHUNK),
        grid=(b, s // ts),
        in_specs=[blk, blk, blk] + [full2(t) for t in tables],
        out_specs=blk,
        out_shape=jax.ShapeDtypeStruct((b, s, c), BF16),
        scratch_shapes=[pltpu.VMEM((RET_WIDTH, RET_WIDTH), F32)],
        compiler_params=pltpu.CompilerParams(
            dimension_semantics=("parallel", "arbitrary"), vmem_limit_bytes=VMEM_LIMIT),
        name="retention",
    )(rq, rk, rv, *tables)


def _moba_kernel(q_ref, k_ref, vt_ref, y_ref, km_ref, s_ref, *, nblk, group):
    hp = pl.program_id(1)
    j = pl.program_id(2)
    blk = MOBA_BLOCK
    lane = lax.broadcasted_iota(jnp.int32, (1, LANES), 1)
    real_lanes = (lane < HEAD_DIM) == (hp < 2)

    @pl.when(j == 0)
    def _():
        for h in range(2):
            rows = []
            for i in range(nblk):
                kb = k_ref[0, h, i * blk:(i + 1) * blk, :].astype(F32)
                rows.append(jnp.sum(kb, axis=0, keepdims=True) * (1.0 / blk))
            km = jnp.where(real_lanes, jnp.concatenate(rows, axis=0), 0.0)
            hi = km.astype(BF16)
            km_ref[h, 0:nblk, :] = hi
            km_ref[h, nblk:2 * nblk, :] = (km - hi.astype(F32)).astype(BF16)

    blk_id = lax.broadcasted_iota(jnp.int32, (nblk, blk), 0)
    kdiff = (lax.broadcasted_iota(jnp.int32, (blk, blk), 0)
             - lax.broadcasted_iota(jnp.int32, (blk, blk), 1))

    def variant(nb):
        outs = []
        for h in range(2):
            q = q_ref[0, h]
            g2 = _dot_nt(km_ref[h], q)
            gate = g2[0:nblk] + g2[nblk:2 * nblk]
            gm = jnp.where(blk_id < j, gate, NEG)
            cnt = jnp.zeros((nblk, blk), jnp.int32)
            for i2 in range(nblk):
                r = gm[i2:i2 + 1, :]
                beats = (r > gm) | ((r == gm) & (blk_id > i2))
                cnt = cnt + beats.astype(jnp.int32)
            sel = ((cnt < MOBA_TOPK) & (blk_id < j)) | (blk_id == j)
            mb = jnp.where(sel, 0.0, NEG)

            m8 = jnp.full((8, blk), NEG, F32)
            for i in range(nb):
                s = _dot_nt(k_ref[0, h, i * blk:(i + 1) * blk, :], q) + mb[i:i + 1, :]
                if i >= nb - group:
                    s = jnp.where(kdiff > (j - i) * blk, NEG, s)
                s_ref[h, i] = s
                m8 = jnp.maximum(m8, jnp.max(s.reshape(blk // 8, 8, blk), axis=0))
            m = jnp.max(m8, axis=0, keepdims=True)
            acc = jnp.zeros((LANES, blk), F32)
            for i in range(nb):
                p = jnp.exp(s_ref[h, i] - m).astype(BF16)
                acc = acc + _dot(vt_ref[0, h, i], p)
            outs.append(acc[0:HEAD_DIM, :] * (1.0 / acc[HEAD_DIM:HEAD_DIM + 1, :]))
        y_ref[0] = jnp.concatenate(outs, axis=0).T.astype(BF16)

    for v in range(nblk // group):
        pl.when(j // group == v)(functools.partial(variant, group * (v + 1)))


def _moba(qa, ka, vt):
    b, nh, s, _ = qa.shape
    nblk = s // MOBA_BLOCK
    group = 2 if nblk % 2 == 0 else 1
    return pl.pallas_call(
        functools.partial(_moba_kernel, nblk=nblk, group=group),
        grid=(b, nh // 2, nblk),
        in_specs=[
            pl.BlockSpec((1, 2, MOBA_BLOCK, LANES), lambda i, p, j: (i, p, j, 0)),
            pl.BlockSpec((1, 2, s, LANES), lambda i, p, j: (i, p, 0, 0)),
            pl.BlockSpec((1, 2, nblk, LANES, MOBA_BLOCK), lambda i, p, j: (i, p, 0, 0, 0)),
        ],
        out_specs=pl.BlockSpec((1, MOBA_BLOCK, LANES), lambda i, p, j: (i, j, p)),
        out_shape=jax.ShapeDtypeStruct((b, s, ATT_WIDTH), BF16),
        scratch_shapes=[
            pltpu.VMEM((2, 2 * nblk, LANES), BF16),
            pltpu.VMEM((2, nblk, MOBA_BLOCK, MOBA_BLOCK), F32),
        ],
        compiler_params=pltpu.CompilerParams(
            dimension_semantics=("parallel", "parallel", "arbitrary"), vmem_limit_bytes=VMEM_LIMIT),
        name="moba",
    )(qa, ka, vt)


def _outproj_kernel(ya_ref, yr_ref, ym_ref, sg_ref, x_ref, w_ref, fg_ref, o_ref, *, final):
    mix = jnp.concatenate([ya_ref[0], yr_ref[0], ym_ref[0]], axis=-1).astype(F32)
    mix = (mix * sg_ref[0].astype(F32)).astype(BF16)
    xn = x_ref[0] + _dot(mix, w_ref[...])
    if final:
        ms = jnp.mean(xn * xn, axis=-1, keepdims=True)
        xn = xn * lax.rsqrt(ms + RMS_EPS) * fg_ref[...]
    o_ref[0] = xn


def _outproj(ya, yr, ym, sg, x, w, fg, *, tm, final):
    b, s, d = x.shape
    row_spec = lambda c: pl.BlockSpec((1, tm, c), lambda i, j: (i, j, 0))
    full2 = lambda a: pl.BlockSpec(a.shape, lambda i, j: (0, 0))
    return pl.pallas_call(
        functools.partial(_outproj_kernel, final=final),
        grid=(b, s // tm),
        in_specs=[row_spec(CONV_WIDTH), row_spec(RET_WIDTH), row_spec(ATT_WIDTH), row_spec(d),
                  row_spec(d), full2(w), full2(fg)],
        out_specs=row_spec(d),
        out_shape=jax.ShapeDtypeStruct((b, s, d), F32),
        compiler_params=pltpu.CompilerParams(
            dimension_semantics=("parallel", "parallel"), vmem_limit_bytes=VMEM_LIMIT),
        name="outproj",
    )(ya, yr, ym, sg, x, w, fg)


def _pack_weights(w_in_l):
    sizes = (CONV_WIDTH,) * 3 + (RET_WIDTH,) * 4 + (ATT_WIDTH,) * 4
    pts = np.cumsum(sizes)[:-1]
    a_val, a_glu, a_gate, r_q, r_k, r_v, r_gate, m_q, m_k, m_v, m_gate = jnp.split(w_in_l, pts, axis=-1)
    perm = np.concatenate([np.r_[g * HEAD_DIM:(g + 1) * HEAD_DIM, (g + 4) * HEAD_DIM:(g + 5) * HEAD_DIM]
                           for g in range(ATT_HEADS // 2)])
    w = jnp.concatenate([a_val, a_glu, a_gate, r_q, r_k, r_v, r_gate, m_q[:, perm], m_k[:, perm], m_gate],
                        axis=-1).astype(BF16)
    return w, m_v.T.astype(BF16)


def _q_bias_table():
    slopes = np.exp2(-8.0 * (np.arange(ATT_HEADS, dtype=np.float64) + 1.0) / ATT_HEADS)
    tab = np.zeros((ATT_HEADS, LANES), np.float32)
    for h in range(ATT_HEADS):
        base = HEAD_DIM if h < ATT_HEADS // 2 else 0
        tab[h, base] = slopes[h]
        tab[h, base + 1] = slopes[h]
    return jnp.asarray(tab)


def kernel(x, norm_g, w_in, conv_w, conv_b, conv_ln_g, conv_ln_b, conv_pw_w, conv_pw_b, w_out, final_g):
    b, s, d = x.shape
    tm = min(512, s)
    qtab = _q_bias_table()
    tables = _ret_tables()
    fg = final_g.reshape(1, d)
    for layer in range(DEPTH):
        w, wvt = _pack_weights(w_in[layer])
        u, sg, rq, rk, rv, qa, ka, vt = _inproj(x, norm_g[layer].reshape(1, d), w, wvt, qtab, tm=tm)
        ya = _conv(u, conv_w[layer], conv_b[layer].reshape(1, -1), conv_ln_g[layer].reshape(1, -1),
                   conv_ln_b[layer].reshape(1, -1), conv_pw_w[layer].astype(BF16),
                   conv_pw_b[layer].reshape(1, -1), ts=tm)
        yr = _retention(rq, rk, rv, tables, ts=tm)
        ym = _moba(qa, ka, vt)
        x = _outproj(ya, yr, ym, sg, x, w_out[layer].astype(BF16), fg, tm=tm, final=(layer == DEPTH - 1))
    return x
```

```python
import functools

import numpy as np
import jax
import jax.numpy as jnp
from jax import lax
from jax.experimental import pallas as pl
from jax.experimental.pallas import tpu as pltpu

D_MODEL = 1024
DEPTH = 2
HEAD_DIM = 64
CONV_WIDTH = 256
RET_WIDTH = 256
ATT_WIDTH = 512
RET_HEADS = RET_WIDTH // HEAD_DIM
ATT_HEADS = ATT_WIDTH // HEAD_DIM
CONV_K = 31
RET_CHUNK = 128
MOBA_BLOCK = 256
MOBA_TOPK = 3
RMS_EPS = 1e-6
LN_EPS = 1e-5
NEG = -1e30

LANES = 128
SUBLANES = 8
CONV_ROWS = 64
HALO = 32
VMEM_LIMIT = 56 * 1024 * 1024

F32 = jnp.float32
BF16 = jnp.bfloat16

_C_AVAL, _C_AGLU, _C_AGATE = 0, 256, 512
_C_RQ, _C_RK, _C_RV, _C_RGATE = 768, 1024, 1280, 1536
_C_MQ, _C_MK, _C_MGATE, _C_END = 1792, 2304, 2816, 3328


def _sigmoid(x):
    return 1.0 / (1.0 + jnp.exp(-x))


def _dot(a, b):
    return jnp.dot(a, b, preferred_element_type=F32)


def _dot_nt(a, b):
    return lax.dot_general(a, b, (((1,), (1,)), ((), ())), preferred_element_type=F32)


def _dot_tn(a, b):
    return lax.dot_general(a, b, (((0,), (0,)), ((), ())), preferred_element_type=F32)


def _inproj_kernel(x_ref, g_ref, w_ref, wvt_ref, qtab_ref,
                   u_ref, sg_ref, rq_ref, rk_ref, rv_ref, qa_ref, ka_ref, vt_ref, hn_ref, *, tm):
    x = x_ref[0]
    ms = jnp.mean(x * x, axis=-1, keepdims=True)
    hn_ref[...] = (x * lax.rsqrt(ms + RMS_EPS) * g_ref[...]).astype(BF16)

    def proj(lo, hi):
        return _dot(hn_ref[...], w_ref[:, lo:hi])

    a = proj(_C_AVAL, _C_AGATE)
    u_ref[0] = (a[:, :CONV_WIDTH] * _sigmoid(a[:, CONV_WIDTH:])).astype(BF16)

    def silu(t):
        return t * _sigmoid(t)

    sg_ref[0, :, 0:256] = silu(proj(_C_AGATE, _C_RQ)).astype(BF16)
    sg_ref[0, :, 256:512] = silu(proj(_C_RGATE, _C_MQ)).astype(BF16)
    sg_ref[0, :, 512:1024] = silu(proj(_C_MGATE, _C_END)).astype(BF16)

    r = proj(_C_RQ, _C_RGATE)
    rq_ref[0] = r[:, 0:256].astype(BF16)
    rk_ref[0] = r[:, 256:512].astype(BF16)
    rv_ref[0] = r[:, 512:768].astype(BF16)

    lane = lax.broadcasted_iota(jnp.int32, (tm, LANES), 1)
    low = lane < HEAD_DIM
    row = lax.broadcasted_iota(jnp.int32, (tm, LANES), 0) + pl.program_id(1) * tm
    pos_lo = (row % MOBA_BLOCK).astype(F32)
    pos_hi = (row - row % MOBA_BLOCK).astype(F32)
    zero = jnp.zeros((tm, LANES), F32)
    kb_lowhead = jnp.where(lane == HEAD_DIM, pos_lo, jnp.where(lane == HEAD_DIM + 1, pos_hi, zero))
    kb_highhead = jnp.where(lane == 0, pos_lo, jnp.where(lane == 1, pos_hi, zero))

    mq = proj(_C_MQ, _C_MK) * (HEAD_DIM ** -0.5)
    mk = proj(_C_MK, _C_MGATE)
    for g in range(ATT_HEADS // 2):
        qg = mq[:, g * LANES:(g + 1) * LANES]
        kg = mk[:, g * LANES:(g + 1) * LANES]
        qa_ref[0, g] = jnp.where(low, qg, qtab_ref[g:g + 1, :]).astype(BF16)
        qa_ref[0, g + 4] = jnp.where(low, qtab_ref[g + 4:g + 5, :], qg).astype(BF16)
        ka_ref[0, g] = jnp.where(low, kg, kb_lowhead).astype(BF16)
        ka_ref[0, g + 4] = jnp.where(low, kb_highhead, kg).astype(BF16)

    vt = _dot_nt(wvt_ref[...], hn_ref[...])
    sub = lax.broadcasted_iota(jnp.int32, (HEAD_DIM, MOBA_BLOCK), 0)
    ones_row = jnp.where(sub == 0, 1.0, 0.0).astype(BF16)
    for h in range(ATT_HEADS):
        for c in range(tm // MOBA_BLOCK):
            vt_ref[0, h, c, 0:HEAD_DIM, :] = vt[h * HEAD_DIM:(h + 1) * HEAD_DIM,
                                                c * MOBA_BLOCK:(c + 1) * MOBA_BLOCK].astype(BF16)
            vt_ref[0, h, c, HEAD_DIM:, :] = ones_row


def _inproj(x, g, w, wvt, qtab, *, tm):
    b, s, d = x.shape
    grid = (b, s // tm)
    row_spec = lambda c: pl.BlockSpec((1, tm, c), lambda i, j: (i, j, 0))
    full2 = lambda a: pl.BlockSpec(a.shape, lambda i, j: (0, 0))
    out_shape = (
        jax.ShapeDtypeStruct((b, s, CONV_WIDTH), BF16),
        jax.ShapeDtypeStruct((b, s, D_MODEL), BF16),
        jax.ShapeDtypeStruct((b, s, RET_WIDTH), BF16),
        jax.ShapeDtypeStruct((b, s, RET_WIDTH), BF16),
        jax.ShapeDtypeStruct((b, s, RET_WIDTH), BF16),
        jax.ShapeDtypeStruct((b, ATT_HEADS, s, LANES), BF16),
        jax.ShapeDtypeStruct((b, ATT_HEADS, s, LANES), BF16),
        jax.ShapeDtypeStruct((b, ATT_HEADS, s // MOBA_BLOCK, LANES, MOBA_BLOCK), BF16),
    )
    out_specs = (
        row_spec(CONV_WIDTH), row_spec(D_MODEL), row_spec(RET_WIDTH), row_spec(RET_WIDTH),
        row_spec(RET_WIDTH),
        pl.BlockSpec((1, ATT_HEADS, tm, LANES), lambda i, j: (i, 0, j, 0)),
        pl.BlockSpec((1, ATT_HEADS, tm, LANES), lambda i, j: (i, 0, j, 0)),
        pl.BlockSpec((1, ATT_HEADS, tm // MOBA_BLOCK, LANES, MOBA_BLOCK), lambda i, j: (i, 0, j, 0, 0)),
    )
    return pl.pallas_call(
        functools.partial(_inproj_kernel, tm=tm),
        grid=grid,
        in_specs=[row_spec(d), full2(g), full2(w), full2(wvt), full2(qtab)],
        out_specs=out_specs,
        out_shape=out_shape,
        scratch_shapes=[pltpu.VMEM((tm, d), BF16)],
        compiler_params=pltpu.CompilerParams(
            dimension_semantics=("parallel", "parallel"), vmem_limit_bytes=VMEM_LIMIT),
        name="inproj",
    )(x, g, w, wvt, qtab)


def _conv_kernel(u_ref, halo_ref, cw_ref, cb_ref, lg_ref, lb_ref, pw_ref, pb_ref, y_ref,
                 ext_ref, sh_ref, sw_ref, *, ts):
    first = pl.program_id(1) == 0
    halo = halo_ref[0].astype(F32)
    ext_ref[0:HALO, :] = jnp.where(first, 0.0, halo)
    ext_ref[HALO:, :] = u_ref[0].astype(F32)
    span = HALO + ts - SUBLANES
    for r in range(1, SUBLANES):
        sh_ref[r - 1] = ext_ref[r:r + span, :]
    base = HALO - (CONV_K - 1)
    for c in range(ts // CONV_ROWS):
        r0 = c * CONV_ROWS
        acc = jnp.zeros((CONV_ROWS, CONV_WIDTH), F32) + cb_ref[...]
        for k in range(CONV_K):
            r = (base + k) % SUBLANES
            a = base + k - r + r0
            win = ext_ref[a:a + CONV_ROWS, :] if r == 0 else sh_ref[r - 1, a:a + CONV_ROWS, :]
            acc = acc + win * cw_ref[k:k + 1, :]
        mu = jnp.mean(acc, axis=-1, keepdims=True)
        dlt = acc - mu
        var = jnp.mean(dlt * dlt, axis=-1, keepdims=True)
        yn = dlt * lax.rsqrt(var + LN_EPS) * lg_ref[...] + lb_ref[...]
        sw_ref[r0:r0 + CONV_ROWS, :] = (yn * _sigmoid(yn)).astype(BF16)
    y_ref[0] = (_dot(sw_ref[...], pw_ref[...]) + pb_ref[...]).astype(BF16)


def _conv(u, cw, cb, lg, lb, pw, pb, *, ts):
    b, s, c = u.shape
    per = ts // HALO
    full2 = lambda a: pl.BlockSpec(a.shape, lambda i, j: (0, 0))
    return pl.pallas_call(
        functools.partial(_conv_kernel, ts=ts),
        grid=(b, s // ts),
        in_specs=[
            pl.BlockSpec((1, ts, c), lambda i, j: (i, j, 0)),
            pl.BlockSpec((1, HALO, c), lambda i, j: (i, jnp.maximum(j * per - 1, 0), 0)),
            full2(cw), full2(cb), full2(lg), full2(lb), full2(pw), full2(pb),
        ],
        out_specs=pl.BlockSpec((1, ts, c), lambda i, j: (i, j, 0)),
        out_shape=jax.ShapeDtypeStruct((b, s, c), BF16),
        scratch_shapes=[
            pltpu.VMEM((HALO + ts, c), F32),
            pltpu.VMEM((SUBLANES - 1, HALO + ts - SUBLANES, c), F32),
            pltpu.VMEM((ts, c), BF16),
        ],
        compiler_params=pltpu.CompilerParams(
            dimension_semantics=("parallel", "parallel"), vmem_limit_bytes=VMEM_LIMIT),
        name="conv",
    )(u, u, cw, cb, lg, lb, pw, pb)


def _ret_tables():
    h = jnp.arange(RET_HEADS, dtype=F32)
    log_g = jnp.log(1.0 - jnp.exp2(-5.0 - h))
    i = jnp.arange(RET_CHUNK, dtype=F32)
    diff = i[:, None] - i[None, :]
    dec = jnp.where(diff >= 0, jnp.exp(log_g[:, None, None] * jnp.maximum(diff, 0.0)), 0.0)
    scale = HEAD_DIM ** -0.5
    dec_stack = (dec * scale).reshape(RET_HEADS * RET_CHUNK, RET_CHUNK)
    zeta = jnp.exp(log_g[:, None] * (RET_CHUNK - 1 - i))
    xi = jnp.exp(log_g[:, None] * (i + 1.0))
    zeta_tab = jnp.repeat(zeta.T, HEAD_DIM, axis=1) * scale
    xi_tab = jnp.repeat(xi.T, HEAD_DIM, axis=1)
    g_chunk = jnp.exp(log_g * RET_CHUNK)
    head_of = jnp.arange(RET_WIDTH) // HEAD_DIM
    same = head_of[:, None] == head_of[None, :]
    gtab = jnp.where(same, g_chunk[head_of][:, None], 0.0).astype(F32)
    bmask = same.astype(F32)
    avg = (same.astype(F32) / HEAD_DIM).astype(BF16)
    return dec_stack, zeta_tab, xi_tab, gtab, bmask, avg


def _group_mean(z, avg):
    hi = z.astype(BF16)
    lo = (z - hi.astype(F32)).astype(BF16)
    return _dot(hi, avg) + _dot(lo, avg)


def _ret_kernel(q_ref, k_ref, v_ref, dec_ref, zeta_ref, xi_ref, gtab_ref, bmask_ref, avg_ref,
                y_ref, state_ref, *, chunks):
    @pl.when(pl.program_id(1) == 0)
    def _():
        state_ref[...] = jnp.zeros_like(state_ref)

    c = RET_CHUNK
    lane_head = lax.broadcasted_iota(jnp.int32, (c, RET_WIDTH), 1) // HEAD_DIM
    avg = avg_ref[...]
    for ci in range(chunks):
        sl = slice(ci * c, (ci + 1) * c)
        q = q_ref[0, sl, :]
        k = k_ref[0, sl, :]
        v = v_ref[0, sl, :]
        qf = q.astype(F32)
        qs = jnp.concatenate([jnp.where(lane_head == h, qf, 0.0).astype(BF16) for h in range(RET_HEADS)],
                             axis=0)
        sc = _dot_nt(qs, k) * dec_ref[...]
        full = _dot(sc.astype(BF16), v)
        intra = jnp.zeros((c, RET_WIDTH), F32)
        for h in range(RET_HEADS):
            intra = jnp.where(lane_head == h, full[h * c:(h + 1) * c, :], intra)
        st = state_ref[...]
        cross = _dot(q, st.astype(BF16)) * xi_ref[...]
        o = intra + cross
        kz = (k.astype(F32) * zeta_ref[...]).astype(BF16)
        kv = _dot_tn(kz, v)
        state_ref[...] = kv * bmask_ref[...] + st * gtab_ref[...]
        mu = _group_mean(o, avg)
        dlt = o - mu
        var = _group_mean(dlt * dlt, avg)
        y_ref[0, sl, :] = (dlt * lax.rsqrt(var + LN_EPS)).astype(BF16)


def _retention(rq, rk, rv, tables, *, ts):
    b, s, c = rq.shape
    blk = pl.BlockSpec((1, ts, c), lambda i, j: (i, j, 0))
    full2 = lambda a: pl.BlockSpec(a.shape, lambda i, j: (0, 0))
    return pl.pallas_call(
        functools.partial(_ret_kernel, chunks=ts // RET_CHUNK),
        grid=(b, s // ts),
        in_specs=[blk, blk, blk] + [full2(t) for t in tables],
        out_specs=blk,
        out_shape=jax.ShapeDtypeStruct((b, s, c), BF16),
        scratch_shapes=[pltpu.VMEM((RET_WIDTH, RET_WIDTH), F32)],
        compiler_params=pltpu.CompilerParams(
            dimension_semantics=("parallel", "arbitrary"), vmem_limit_bytes=VMEM_LIMIT),
        name="retention",
    )(rq, rk, rv, *tables)


def _moba_kernel(q_ref, k_ref, vt_ref, y_ref, km_ref, s_ref, m_ref, mb_ref, *, nblk, qt):
    head = pl.program_id(1)
    t = pl.program_id(2)
    blk = MOBA_BLOCK
    nq = qt * blk
    lane = lax.broadcasted_iota(jnp.int32, (1, LANES), 1)
    real_lanes = (lane < HEAD_DIM) == (head < ATT_HEADS // 2)

    @pl.when(t == 0)
    def _():
        rows = []
        for i in range(nblk):
            kb = k_ref[0, 0, i * blk:(i + 1) * blk, :].astype(F32)
            rows.append(jnp.sum(kb, axis=0, keepdims=True) * (1.0 / blk))
        km = jnp.where(real_lanes, jnp.concatenate(rows, axis=0), 0.0)
        hi = km.astype(BF16)
        km_ref[0:nblk, :] = hi
        km_ref[nblk:2 * nblk, :] = (km - hi.astype(F32)).astype(BF16)

    blk_id = lax.broadcasted_iota(jnp.int32, (nblk, nq), 0)
    q_blk = lax.broadcasted_iota(jnp.int32, (nblk, nq), 1) // blk
    kdiff = (lax.broadcasted_iota(jnp.int32, (blk, blk), 0)
             - lax.broadcasted_iota(jnp.int32, (blk, blk), 1))

    def logits_stage(tt, slot):
        j0 = tt * qt
        q_all = q_ref[0, 0]
        g2 = _dot_nt(km_ref[...], q_all)
        gate = g2[0:nblk] + g2[nblk:2 * nblk]
        past = blk_id < q_blk + j0
        gm = jnp.where(past, gate, NEG)
        cnt = jnp.zeros((nblk, nq), jnp.int32)
        for i2 in range(nblk):
            r = gm[i2:i2 + 1, :]
            beats = (r > gm) | ((r == gm) & (blk_id > i2))
            cnt = cnt + beats.astype(jnp.int32)
        sel = ((cnt < MOBA_TOPK) & past) | (blk_id == q_blk + j0)
        mb_all = jnp.where(sel, 0.0, NEG)
        mb_ref[slot] = mb_all

        off = 0
        for jl in range(qt):
            j = j0 + jl
            q = q_all[jl * blk:(jl + 1) * blk, :]
            mb = mb_all[:, jl * blk:(jl + 1) * blk]
            m8 = jnp.full((8, blk), NEG, F32)
            for i in range(j + 1):
                s = _dot_nt(k_ref[0, 0, i * blk:(i + 1) * blk, :], q)
                if i == j:
                    s = jnp.where(kdiff > 0, NEG, s)
                s_ref[slot, off + i] = s
                m8 = jnp.maximum(m8, jnp.max(s.reshape(blk // 8, 8, blk), axis=0) + mb[i:i + 1, :])
                if i == j:
                    m_ref[slot, jl] = jnp.max(m8, axis=0, keepdims=True)
                yield
            off += j + 1

    def value_stage(tt, slot):
        j0 = tt * qt
        off = 0
        for jl in range(qt):
            j = j0 + jl
            m = m_ref[slot, jl]
            acc = jnp.zeros((LANES, blk), F32)
            for i in range(j + 1):
                shift = m - mb_ref[slot, i:i + 1, jl * blk:(jl + 1) * blk]
                p = jnp.exp(s_ref[slot, off + i] - shift).astype(BF16)
                acc = acc + _dot(vt_ref[0, 0, i], p)
                if i == j:
                    o = acc[0:HEAD_DIM, :] * (1.0 / acc[HEAD_DIM:HEAD_DIM + 1, :])
                    y_ref[0, 0, :, jl * blk:(jl + 1) * blk] = o.astype(BF16)
                yield
            off += j + 1

    n_tiles = nblk // qt

    def n_pairs(tt):
        return sum(tt * qt + jl + 1 for jl in range(qt))

    def step(st):
        streams = []
        if st < n_tiles:
            streams.append((logits_stage(n_tiles - 1 - st, st % 2), n_pairs(n_tiles - 1 - st)))
        if st >= 1:
            streams.append((value_stage(n_tiles - st, (st - 1) % 2), n_pairs(n_tiles - st)))
        done = [0] * len(streams)
        while any(d < n for d, (_, n) in zip(done, streams)):
            k = min((d / n, idx) for idx, (d, (_, n)) in enumerate(zip(done, streams)) if d < n)[1]
            next(streams[k][0])
            done[k] += 1
        for gen, _ in streams:
            for _ in gen:
                pass

    for st in range(n_tiles + 1):
        pl.when(t == st)(functools.partial(step, st))


def _moba(qa, ka, vt):
    b, nh, s, _ = qa.shape
    nblk = s // MOBA_BLOCK
    qt = min(4, nblk)
    n_tiles = nblk // qt
    n_logit_blocks = qt * nblk - qt * (qt - 1) // 2
    return pl.pallas_call(
        functools.partial(_moba_kernel, nblk=nblk, qt=qt),
        grid=(b, nh, n_tiles + 1),
        in_specs=[
            pl.BlockSpec((1, 1, qt * MOBA_BLOCK, LANES), lambda i, h, t: (i, h, jnp.maximum(n_tiles - 1 - t, 0), 0)),
            pl.BlockSpec((1, 1, s, LANES), lambda i, h, t: (i, h, 0, 0)),
            pl.BlockSpec((1, 1, nblk, LANES, MOBA_BLOCK), lambda i, h, t: (i, h, 0, 0, 0)),
        ],
        out_specs=pl.BlockSpec((1, 1, HEAD_DIM, qt * MOBA_BLOCK),
                               lambda i, h, t: (i, h, 0, jnp.minimum(n_tiles - t, n_tiles - 1))),
        out_shape=jax.ShapeDtypeStruct((b, nh, HEAD_DIM, s), BF16),
        scratch_shapes=[
            pltpu.VMEM((2 * nblk, LANES), BF16),
            pltpu.VMEM((2, n_logit_blocks, MOBA_BLOCK, MOBA_BLOCK), F32),
            pltpu.VMEM((2, qt, 1, MOBA_BLOCK), F32),
            pltpu.VMEM((2, nblk, qt * MOBA_BLOCK), F32),
        ],
        compiler_params=pltpu.CompilerParams(
            dimension_semantics=("parallel", "parallel", "arbitrary"), vmem_limit_bytes=VMEM_LIMIT),
        name="moba",
    )(qa, ka, vt)


def _outproj_kernel(ya_ref, yr_ref, ymt_ref, sg_ref, x_ref, w_ref, fg_ref, o_ref, *, final):
    tm = x_ref.shape[1]
    ym = ymt_ref[0].reshape(ATT_WIDTH, tm).astype(F32).T
    mix = jnp.concatenate([ya_ref[0].astype(F32), yr_ref[0].astype(F32), ym], axis=-1)
    mix = (mix * sg_ref[0].astype(F32)).astype(BF16)
    xn = x_ref[0] + _dot(mix, w_ref[...])
    if final:
        ms = jnp.mean(xn * xn, axis=-1, keepdims=True)
        xn = xn * lax.rsqrt(ms + RMS_EPS) * fg_ref[...]
    o_ref[0] = xn


def _outproj(ya, yr, ym, sg, x, w, fg, *, tm, final):
    b, s, d = x.shape
    row_spec = lambda c: pl.BlockSpec((1, tm, c), lambda i, j: (i, j, 0))
    full2 = lambda a: pl.BlockSpec(a.shape, lambda i, j: (0, 0))
    return pl.pallas_call(
        functools.partial(_outproj_kernel, final=final),
        grid=(b, s // tm),
        in_specs=[row_spec(CONV_WIDTH), row_spec(RET_WIDTH),
                  pl.BlockSpec((1, ATT_HEADS, HEAD_DIM, tm), lambda i, j: (i, 0, 0, j)),
                  row_spec(d), row_spec(d), full2(w), full2(fg)],
        out_specs=row_spec(d),
        out_shape=jax.ShapeDtypeStruct((b, s, d), F32),
        compiler_params=pltpu.CompilerParams(
            dimension_semantics=("parallel", "parallel"), vmem_limit_bytes=VMEM_LIMIT),
        name="outproj",
    )(ya, yr, ym, sg, x, w, fg)


def _pack_weights(w_in_l):
    sizes = (CONV_WIDTH,) * 3 + (RET_WIDTH,) * 4 + (ATT_WIDTH,) * 4
    pts = np.cumsum(sizes)[:-1]
    a_val, a_glu, a_gate, r_q, r_k, r_v, r_gate, m_q, m_k, m_v, m_gate = jnp.split(w_in_l, pts, axis=-1)
    perm = np.concatenate([np.r_[g * HEAD_DIM:(g + 1) * HEAD_DIM, (g + 4) * HEAD_DIM:(g + 5) * HEAD_DIM]
                           for g in range(ATT_HEADS // 2)])
    w = jnp.concatenate([a_val, a_glu, a_gate, r_q, r_k, r_v, r_gate, m_q[:, perm], m_k[:, perm], m_gate],
                        axis=-1).astype(BF16)
    return w, m_v.T.astype(BF16)


def _q_bias_table():
    slopes = np.exp2(-8.0 * (np.arange(ATT_HEADS, dtype=np.float64) + 1.0) / ATT_HEADS)
    tab = np.zeros((ATT_HEADS, LANES), np.float32)
    for h in range(ATT_HEADS):
        base = HEAD_DIM if h < ATT_HEADS // 2 else 0
        tab[h, base] = slopes[h]
        tab[h, base + 1] = slopes[h]
    return jnp.asarray(tab)


def kernel(x, norm_g, w_in, conv_w, conv_b, conv_ln_g, conv_ln_b, conv_pw_w, conv_pw_b, w_out, final_g):
    b, s, d = x.shape
    tm = min(512, s)
    qtab = _q_bias_table()
    tables = _ret_tables()
    fg = final_g.reshape(1, d)
    for layer in range(DEPTH):
        w, wvt = _pack_weights(w_in[layer])
        u, sg, rq, rk, rv, qa, ka, vt = _inproj(x, norm_g[layer].reshape(1, d), w, wvt, qtab, tm=tm)
        ya = _conv(u, conv_w[layer], conv_b[layer].reshape(1, -1), conv_ln_g[layer].reshape(1, -1),
                   conv_ln_b[layer].reshape(1, -1), conv_pw_w[layer].astype(BF16),
                   conv_pw_b[layer].reshape(1, -1), ts=tm)
        yr = _retention(rq, rk, rv, tables, ts=tm)
        ym = _moba(qa, ka, vt)
        x = _outproj(ya, yr, ym, sg, x, w_out[layer].astype(BF16), fg, tm=tm, final=(layer == DEPTH - 1))
    return x
```

```python
import functools

import numpy as np
import jax
import jax.numpy as jnp
from jax import lax
from jax.experimental import pallas as pl
from jax.experimental.pallas import tpu as pltpu

D_MODEL = 1024
DEPTH = 2
HEAD_DIM = 64
CONV_WIDTH = 256
RET_WIDTH = 256
ATT_WIDTH = 512
RET_HEADS = RET_WIDTH // HEAD_DIM
ATT_HEADS = ATT_WIDTH // HEAD_DIM
CONV_K = 31
RET_CHUNK = 128
MOBA_BLOCK = 256
MOBA_TOPK = 3
RMS_EPS = 1e-6
LN_EPS = 1e-5
NEG = -1e30

LANES = 128
SUBLANES = 8
CONV_ROWS = 64
HALO = 32
VMEM_LIMIT = 56 * 1024 * 1024

F32 = jnp.float32
BF16 = jnp.bfloat16

_C_AVAL, _C_AGLU, _C_AGATE = 0, 256, 512
_C_RQ, _C_RK, _C_RV, _C_RGATE = 768, 1024, 1280, 1536
_C_MQ, _C_MK, _C_MGATE, _C_END = 1792, 2304, 2816, 3328


def _sigmoid(x):
    return 1.0 / (1.0 + jnp.exp(-x))


def _dot(a, b):
    return jnp.dot(a, b, preferred_element_type=F32)


def _dot_nt(a, b):
    return lax.dot_general(a, b, (((1,), (1,)), ((), ())), preferred_element_type=F32)


def _dot_tn(a, b):
    return lax.dot_general(a, b, (((0,), (0,)), ((), ())), preferred_element_type=F32)


def _ret_tables():
    h = jnp.arange(RET_HEADS, dtype=F32)
    log_g = jnp.log(1.0 - jnp.exp2(-5.0 - h))
    i = jnp.arange(RET_CHUNK, dtype=F32)
    diff = i[:, None] - i[None, :]
    dec = jnp.where(diff >= 0, jnp.exp(log_g[:, None, None] * jnp.maximum(diff, 0.0)), 0.0)
    scale = HEAD_DIM ** -0.5
    dec_stack = (dec * scale).reshape(RET_HEADS * RET_CHUNK, RET_CHUNK)
    zeta = jnp.exp(log_g[:, None] * (RET_CHUNK - 1 - i))
    xi = jnp.exp(log_g[:, None] * (i + 1.0))
    zeta_tab = jnp.repeat(zeta.T, HEAD_DIM, axis=1) * scale
    xi_tab = jnp.repeat(xi.T, HEAD_DIM, axis=1)
    g_chunk = jnp.exp(log_g * RET_CHUNK)
    head_of = jnp.arange(RET_WIDTH) // HEAD_DIM
    same = head_of[:, None] == head_of[None, :]
    gtab = jnp.where(same, g_chunk[head_of][:, None], 0.0).astype(F32)
    bmask = same.astype(F32)
    avg = (same.astype(F32) / HEAD_DIM).astype(BF16)
    return dec_stack, zeta_tab, xi_tab, gtab, bmask, avg


def _group_mean(z, avg):
    hi = z.astype(BF16)
    lo = (z - hi.astype(F32)).astype(BF16)
    return _dot(hi, avg) + _dot(lo, avg)


def _front_kernel(x_ref, g_ref, w_ref, wvt_ref, qtab_ref,
                  cw_ref, cb_ref, lg_ref, lb_ref, pw_ref, pb_ref,
                  dec_ref, zeta_ref, xi_ref, gtab_ref, bmask_ref, avg_ref,
                  ya_ref, yr_ref, sg_ref, qa_ref, ka_ref, vt_ref,
                  hn_ref, ext_ref, sh_ref, sw_ref, r_ref, state_ref, *, tm):
    first = pl.program_id(1) == 0
    x = x_ref[0]
    ms = jnp.mean(x * x, axis=-1, keepdims=True)
    hn_ref[...] = (x * lax.rsqrt(ms + RMS_EPS) * g_ref[...]).astype(BF16)

    def proj(lo, hi):
        return _dot(hn_ref[...], w_ref[:, lo:hi])

    @pl.when(first)
    def _():
        ext_ref[0:HALO, :] = jnp.zeros((HALO, CONV_WIDTH), F32)
        state_ref[...] = jnp.zeros_like(state_ref)

    @pl.when(jnp.logical_not(first))
    def _():
        ext_ref[0:HALO, :] = ext_ref[tm:tm + HALO, :]

    a = proj(_C_AVAL, _C_AGATE)
    ext_ref[HALO:, :] = (a[:, :CONV_WIDTH] * _sigmoid(a[:, CONV_WIDTH:])).astype(BF16).astype(F32)
    span = HALO + tm - SUBLANES
    for r in range(1, SUBLANES):
        sh_ref[r - 1] = ext_ref[r:r + span, :]
    r_ref[...] = proj(_C_RQ, _C_RGATE).astype(BF16)

    def conv_chunk(c):
        base = HALO - (CONV_K - 1)
        r0 = c * CONV_ROWS
        acc = jnp.zeros((CONV_ROWS, CONV_WIDTH), F32) + cb_ref[...]
        for k in range(CONV_K):
            r = (base + k) % SUBLANES
            a0 = base + k - r + r0
            win = ext_ref[a0:a0 + CONV_ROWS, :] if r == 0 else sh_ref[r - 1, a0:a0 + CONV_ROWS, :]
            acc = acc + win * cw_ref[k:k + 1, :]
        mu = jnp.mean(acc, axis=-1, keepdims=True)
        dlt = acc - mu
        var = jnp.mean(dlt * dlt, axis=-1, keepdims=True)
        yn = dlt * lax.rsqrt(var + LN_EPS) * lg_ref[...] + lb_ref[...]
        sw_ref[r0:r0 + CONV_ROWS, :] = (yn * _sigmoid(yn)).astype(BF16)

    def conv_out():
        ya_ref[0] = (_dot(sw_ref[...], pw_ref[...]) + pb_ref[...]).astype(BF16)

    c = RET_CHUNK
    lane_head = lax.broadcasted_iota(jnp.int32, (c, RET_WIDTH), 1) // HEAD_DIM

    def retention():
        chunks = range(tm // c)
        sls = [slice(ci * c, (ci + 1) * c) for ci in chunks]
        avg = avg_ref[...]
        probs, kzs = [], []
        for sl in sls:
            qf = r_ref[sl, 0:RET_WIDTH].astype(F32)
            k = r_ref[sl, RET_WIDTH:2 * RET_WIDTH]
            qs = jnp.concatenate([jnp.where(lane_head == h, qf, 0.0).astype(BF16) for h in range(RET_HEADS)],
                                 axis=0)
            probs.append((_dot_nt(qs, k) * dec_ref[...]).astype(BF16))
            kzs.append((k.astype(F32) * zeta_ref[...]).astype(BF16))
        yield
        intras, states = [], []
        st = state_ref[...]
        for sl, p, kz in zip(sls, probs, kzs):
            v = r_ref[sl, 2 * RET_WIDTH:3 * RET_WIDTH]
            full = _dot(p, v)
            intra = jnp.zeros((c, RET_WIDTH), F32)
            for h in range(RET_HEADS):
                intra = jnp.where(lane_head == h, full[h * c:(h + 1) * c, :], intra)
            intras.append(intra)
            states.append(st.astype(BF16))
            st = _dot_tn(kz, v) * bmask_ref[...] + st * gtab_ref[...]
        state_ref[...] = st
        yield
        outs, mus = [], []
        for sl, intra, stb in zip(sls, intras, states):
            o = intra + _dot(r_ref[sl, 0:RET_WIDTH], stb) * xi_ref[...]
            outs.append(o)
            mus.append(_group_mean(o, avg))
        yield
        for sl, o, mu in zip(sls, outs, mus):
            dlt = o - mu
            var = _group_mean(dlt * dlt, avg)
            yr_ref[0, sl, :] = (dlt * lax.rsqrt(var + LN_EPS)).astype(BF16)
        yield

    def silu(t):
        return t * _sigmoid(t)

    def gates():
        sg_ref[0, :, 0:256] = silu(proj(_C_AGATE, _C_RQ)).astype(BF16)
        sg_ref[0, :, 256:512] = silu(proj(_C_RGATE, _C_MQ)).astype(BF16)

    def gate_m():
        sg_ref[0, :, 512:1024] = silu(proj(_C_MGATE, _C_END)).astype(BF16)

    def moba_qk():
        lane = lax.broadcasted_iota(jnp.int32, (tm, LANES), 1)
        low = lane < HEAD_DIM
        row = lax.broadcasted_iota(jnp.int32, (tm, LANES), 0) + pl.program_id(1) * tm
        pos_lo = (row % MOBA_BLOCK).astype(F32)
        pos_hi = (row - row % MOBA_BLOCK).astype(F32)
        zero = jnp.zeros((tm, LANES), F32)
        kb_lowhead = jnp.where(lane == HEAD_DIM, pos_lo, jnp.where(lane == HEAD_DIM + 1, pos_hi, zero))
        kb_highhead = jnp.where(lane == 0, pos_lo, jnp.where(lane == 1, pos_hi, zero))
        mq = proj(_C_MQ, _C_MK) * (HEAD_DIM ** -0.5)
        mk = proj(_C_MK, _C_MGATE)
        for g in range(ATT_HEADS // 2):
            qg = mq[:, g * LANES:(g + 1) * LANES]
            kg = mk[:, g * LANES:(g + 1) * LANES]
            qa_ref[0, g] = jnp.where(low, qg, qtab_ref[g:g + 1, :]).astype(BF16)
            qa_ref[0, g + 4] = jnp.where(low, qtab_ref[g + 4:g + 5, :], qg).astype(BF16)
            ka_ref[0, g] = jnp.where(low, kg, kb_lowhead).astype(BF16)
            ka_ref[0, g + 4] = jnp.where(low, kb_highhead, kg).astype(BF16)

    def moba_v():
        vt = _dot_nt(wvt_ref[...], hn_ref[...])
        sub = lax.broadcasted_iota(jnp.int32, (HEAD_DIM, MOBA_BLOCK), 0)
        ones_row = jnp.where(sub == 0, 1.0, 0.0).astype(BF16)
        for h in range(ATT_HEADS):
            for cb in range(tm // MOBA_BLOCK):
                vt_ref[0, h, cb, 0:HEAD_DIM, :] = vt[h * HEAD_DIM:(h + 1) * HEAD_DIM,
                                                     cb * MOBA_BLOCK:(cb + 1) * MOBA_BLOCK].astype(BF16)
                vt_ref[0, h, cb, HEAD_DIM:, :] = ones_row

    n_conv = tm // CONV_ROWS
    mxu_work = [gates, gate_m, moba_qk, moba_v]
    ret_stages = retention()
    n_slots = len(mxu_work)
    for slot in range(n_slots):
        next(ret_stages)
        for cc in range(slot * n_conv // n_slots, (slot + 1) * n_conv // n_slots):
            conv_chunk(cc)
        mxu_work[slot]()
    conv_out()


def _front(x, g, w, wvt, qtab, conv_params, tables, *, tm):
    b, s, d = x.shape
    row_spec = lambda c: pl.BlockSpec((1, tm, c), lambda i, j: (i, j, 0))
    full2 = lambda a: pl.BlockSpec(a.shape, lambda i, j: (0, 0))
    consts = (g, w, wvt, qtab) + tuple(conv_params) + tuple(tables)
    out_shape = (
        jax.ShapeDtypeStruct((b, s, CONV_WIDTH), BF16),
        jax.ShapeDtypeStruct((b, s, RET_WIDTH), BF16),
        jax.ShapeDtypeStruct((b, s, D_MODEL), BF16),
        jax.ShapeDtypeStruct((b, ATT_HEADS, s, LANES), BF16),
        jax.ShapeDtypeStruct((b, ATT_HEADS, s, LANES), BF16),
        jax.ShapeDtypeStruct((b, ATT_HEADS, s // MOBA_BLOCK, LANES, MOBA_BLOCK), BF16),
    )
    out_specs = (
        row_spec(CONV_WIDTH), row_spec(RET_WIDTH), row_spec(D_MODEL),
        pl.BlockSpec((1, ATT_HEADS, tm, LANES), lambda i, j: (i, 0, j, 0)),
        pl.BlockSpec((1, ATT_HEADS, tm, LANES), lambda i, j: (i, 0, j, 0)),
        pl.BlockSpec((1, ATT_HEADS, tm // MOBA_BLOCK, LANES, MOBA_BLOCK), lambda i, j: (i, 0, j, 0, 0)),
    )
    return pl.pallas_call(
        functools.partial(_front_kernel, tm=tm),
        grid=(b, s // tm),
        in_specs=[row_spec(d)] + [full2(t) for t in consts],
        out_specs=out_specs,
        out_shape=out_shape,
        scratch_shapes=[
            pltpu.VMEM((tm, d), BF16),
            pltpu.VMEM((HALO + tm, CONV_WIDTH), F32),
            pltpu.VMEM((SUBLANES - 1, HALO + tm - SUBLANES, CONV_WIDTH), F32),
            pltpu.VMEM((tm, CONV_WIDTH), BF16),
            pltpu.VMEM((tm, 3 * RET_WIDTH), BF16),
            pltpu.VMEM((RET_WIDTH, RET_WIDTH), F32),
        ],
        compiler_params=pltpu.CompilerParams(
            dimension_semantics=("parallel", "arbitrary"), vmem_limit_bytes=VMEM_LIMIT),
        name="front",
    )(x, *consts)


def _moba_kernel(q_ref, k_ref, vt_ref, y_ref, km_ref, s_ref, m_ref, mb_ref, *, nblk, qt):
    head = pl.program_id(1)
    t = pl.program_id(2)
    blk = MOBA_BLOCK
    nq = qt * blk
    lane = lax.broadcasted_iota(jnp.int32, (1, LANES), 1)
    real_lanes = (lane < HEAD_DIM) == (head < ATT_HEADS // 2)

    @pl.when(t == 0)
    def _():
        rows = []
        for i in range(nblk):
            kb = k_ref[0, 0, i * blk:(i + 1) * blk, :].astype(F32)
            rows.append(jnp.sum(kb, axis=0, keepdims=True) * (1.0 / blk))
        km = jnp.where(real_lanes, jnp.concatenate(rows, axis=0), 0.0)
        hi = km.astype(BF16)
        km_ref[0:nblk, :] = hi
        km_ref[nblk:2 * nblk, :] = (km - hi.astype(F32)).astype(BF16)

    blk_id = lax.broadcasted_iota(jnp.int32, (nblk, nq), 0)
    q_blk = lax.broadcasted_iota(jnp.int32, (nblk, nq), 1) // blk
    kdiff = (lax.broadcasted_iota(jnp.int32, (blk, blk), 0)
             - lax.broadcasted_iota(jnp.int32, (blk, blk), 1))

    def logits_stage(tt, slot):
        j0 = tt * qt
        q_all = q_ref[0, 0]
        g2 = _dot_nt(km_ref[...], q_all)
        gate = g2[0:nblk] + g2[nblk:2 * nblk]
        past = blk_id < q_blk + j0
        gm = jnp.where(past, gate, NEG)
        cnt = jnp.zeros((nblk, nq), jnp.int32)
        for i2 in range(nblk):
            r = gm[i2:i2 + 1, :]
            beats = (r > gm) | ((r == gm) & (blk_id > i2))
            cnt = cnt + beats.astype(jnp.int32)
        sel = ((cnt < MOBA_TOPK) & past) | (blk_id == q_blk + j0)
        mb_all = jnp.where(sel, 0.0, NEG)
        mb_ref[slot] = mb_all

        off = 0
        for jl in range(qt):
            j = j0 + jl
            q = q_all[jl * blk:(jl + 1) * blk, :]
            mb = mb_all[:, jl * blk:(jl + 1) * blk]
            m8 = jnp.full((8, blk), NEG, F32)
            for i in range(j + 1):
                s = _dot_nt(k_ref[0, 0, i * blk:(i + 1) * blk, :], q)
                if i == j:
                    s = jnp.where(kdiff > 0, NEG, s)
                s_ref[slot, off + i] = s
                m8 = jnp.maximum(m8, jnp.max(s.reshape(blk // 8, 8, blk), axis=0) + mb[i:i + 1, :])
                if i == j:
                    m_ref[slot, jl] = jnp.max(m8, axis=0, keepdims=True)
                yield
            off += j + 1

    def value_stage(tt, slot):
        j0 = tt * qt
        off = 0
        for jl in range(qt):
            j = j0 + jl
            m = m_ref[slot, jl]
            acc = jnp.zeros((LANES, blk), F32)
            for i in range(j + 1):
                shift = m - mb_ref[slot, i:i + 1, jl * blk:(jl + 1) * blk]
                p = jnp.exp(s_ref[slot, off + i] - shift).astype(BF16)
                acc = acc + _dot(vt_ref[0, 0, i], p)
                if i == j:
                    o = acc[0:HEAD_DIM, :] * (1.0 / acc[HEAD_DIM:HEAD_DIM + 1, :])
                    y_ref[0, 0, :, jl * blk:(jl + 1) * blk] = o.astype(BF16)
                yield
            off += j + 1

    n_tiles = nblk // qt

    def n_pairs(tt):
        return sum(tt * qt + jl + 1 for jl in range(qt))

    def step(st):
        streams = []
        if st < n_tiles:
            streams.append((logits_stage(n_tiles - 1 - st, st % 2), n_pairs(n_tiles - 1 - st)))
        if st >= 1:
            streams.append((value_stage(n_tiles - st, (st - 1) % 2), n_pairs(n_tiles - st)))
        done = [0] * len(streams)
        while any(d < n for d, (_, n) in zip(done, streams)):
            k = min((d / n, idx) for idx, (d, (_, n)) in enumerate(zip(done, streams)) if d < n)[1]
            next(streams[k][0])
            done[k] += 1
        for gen, _ in streams:
            for _ in gen:
                pass

    for st in range(n_tiles + 1):
        pl.when(t == st)(functools.partial(step, st))


def _moba(qa, ka, vt):
    b, nh, s, _ = qa.shape
    nblk = s // MOBA_BLOCK
    qt = min(4, nblk)
    n_tiles = nblk // qt
    n_logit_blocks = qt * nblk - qt * (qt - 1) // 2
    return pl.pallas_call(
        functools.partial(_moba_kernel, nblk=nblk, qt=qt),
        grid=(b, nh, n_tiles + 1),
        in_specs=[
            pl.BlockSpec((1, 1, qt * MOBA_BLOCK, LANES), lambda i, h, t: (i, h, jnp.maximum(n_tiles - 1 - t, 0), 0)),
            pl.BlockSpec((1, 1, s, LANES), lambda i, h, t: (i, h, 0, 0)),
            pl.BlockSpec((1, 1, nblk, LANES, MOBA_BLOCK), lambda i, h, t: (i, h, 0, 0, 0)),
        ],
        out_specs=pl.BlockSpec((1, 1, HEAD_DIM, qt * MOBA_BLOCK),
                               lambda i, h, t: (i, h, 0, jnp.minimum(n_tiles - t, n_tiles - 1))),
        out_shape=jax.ShapeDtypeStruct((b, nh, HEAD_DIM, s), BF16),
        scratch_shapes=[
            pltpu.VMEM((2 * nblk, LANES), BF16),
            pltpu.VMEM((2, n_logit_blocks, MOBA_BLOCK, MOBA_BLOCK), F32),
            pltpu.VMEM((2, qt, 1, MOBA_BLOCK), F32),
            pltpu.VMEM((2, nblk, qt * MOBA_BLOCK), F32),
        ],
        compiler_params=pltpu.CompilerParams(
            dimension_semantics=("parallel", "parallel", "arbitrary"), vmem_limit_bytes=VMEM_LIMIT),
        name="moba",
    )(qa, ka, vt)


def _outproj_kernel(ya_ref, yr_ref, ymt_ref, sg_ref, x_ref, w_ref, fg_ref, o_ref, *, final):
    tm = x_ref.shape[1]
    ym = ymt_ref[0].reshape(ATT_WIDTH, tm).astype(F32).T
    mix = jnp.concatenate([ya_ref[0].astype(F32), yr_ref[0].astype(F32), ym], axis=-1)
    mix = (mix * sg_ref[0].astype(F32)).astype(BF16)
    xn = x_ref[0] + _dot(mix, w_ref[...])
    if final:
        ms = jnp.mean(xn * xn, axis=-1, keepdims=True)
        xn = xn * lax.rsqrt(ms + RMS_EPS) * fg_ref[...]
    o_ref[0] = xn


def _outproj(ya, yr, ym, sg, x, w, fg, *, tm, final):
    b, s, d = x.shape
    row_spec = lambda c: pl.BlockSpec((1, tm, c), lambda i, j: (i, j, 0))
    full2 = lambda a: pl.BlockSpec(a.shape, lambda i, j: (0, 0))
    return pl.pallas_call(
        functools.partial(_outproj_kernel, final=final),
        grid=(b, s // tm),
        in_specs=[row_spec(CONV_WIDTH), row_spec(RET_WIDTH),
                  pl.BlockSpec((1, ATT_HEADS, HEAD_DIM, tm), lambda i, j: (i, 0, 0, j)),
                  row_spec(d), row_spec(d), full2(w), full2(fg)],
        out_specs=row_spec(d),
        out_shape=jax.ShapeDtypeStruct((b, s, d), F32),
        compiler_params=pltpu.CompilerParams(
            dimension_semantics=("parallel", "parallel"), vmem_limit_bytes=VMEM_LIMIT),
        name="outproj",
    )(ya, yr, ym, sg, x, w, fg)


def _pack_weights(w_in_l):
    sizes = (CONV_WIDTH,) * 3 + (RET_WIDTH,) * 4 + (ATT_WIDTH,) * 4
    pts = np.cumsum(sizes)[:-1]
    a_val, a_glu, a_gate, r_q, r_k, r_v, r_gate, m_q, m_k, m_v, m_gate = jnp.split(w_in_l, pts, axis=-1)
    perm = np.concatenate([np.r_[g * HEAD_DIM:(g + 1) * HEAD_DIM, (g + 4) * HEAD_DIM:(g + 5) * HEAD_DIM]
                           for g in range(ATT_HEADS // 2)])
    w = jnp.concatenate([a_val, a_glu, a_gate, r_q, r_k, r_v, r_gate, m_q[:, perm], m_k[:, perm], m_gate],
                        axis=-1).astype(BF16)
    return w, m_v.T.astype(BF16)


def _q_bias_table():
    slopes = np.exp2(-8.0 * (np.arange(ATT_HEADS, dtype=np.float64) + 1.0) / ATT_HEADS)
    tab = np.zeros((ATT_HEADS, LANES), np.float32)
    for h in range(ATT_HEADS):
        base = HEAD_DIM if h < ATT_HEADS // 2 else 0
        tab[h, base] = slopes[h]
        tab[h, base + 1] = slopes[h]
    return jnp.asarray(tab)


def kernel(x, norm_g, w_in, conv_w, conv_b, conv_ln_g, conv_ln_b, conv_pw_w, conv_pw_b, w_out, final_g):
    b, s, d = x.shape
    tm = min(512, s)
    qtab = _q_bias_table()
    tables = _ret_tables()
    fg = final_g.reshape(1, d)
    for layer in range(DEPTH):
        w, wvt = _pack_weights(w_in[layer])
        conv_params = (conv_w[layer], conv_b[layer].reshape(1, -1), conv_ln_g[layer].reshape(1, -1),
                       conv_ln_b[layer].reshape(1, -1), conv_pw_w[layer].astype(BF16),
                       conv_pw_b[layer].reshape(1, -1))
        ya, yr, sg, qa, ka, vt = _front(x, norm_g[layer].reshape(1, d), w, wvt, qtab, conv_params, tables, tm=tm)
        ym = _moba(qa, ka, vt)
        x = _outproj(ya, yr, ym, sg, x, w_out[layer].astype(BF16), fg, tm=tm, final=(layer == DEPTH - 1))
    return x
```

```python
import functools

import numpy as np
import jax
import jax.numpy as jnp
from jax import lax
from jax.experimental import pallas as pl
from jax.experimental.pallas import tpu as pltpu

D_MODEL = 1024
DEPTH = 2
HEAD_DIM = 64
CONV_WIDTH = 256
RET_WIDTH = 256
ATT_WIDTH = 512
RET_HEADS = RET_WIDTH // HEAD_DIM
ATT_HEADS = ATT_WIDTH // HEAD_DIM
CONV_K = 31
RET_CHUNK = 128
MOBA_BLOCK = 256
MOBA_TOPK = 3
RMS_EPS = 1e-6
LN_EPS = 1e-5
NEG = -1e30
LOG2E = 1.4426950408889634
BIAS_TERMS = 3

LANES = 128
SUBLANES = 8
VT_ROWS = HEAD_DIM + 16
CONV_ROWS = 64
HALO = 32
VMEM_LIMIT = 56 * 1024 * 1024

F32 = jnp.float32
BF16 = jnp.bfloat16

_C_AVAL, _C_AGLU, _C_AGATE = 0, 256, 512
_C_RQ, _C_RK, _C_RV, _C_RGATE = 768, 1024, 1280, 1536
_C_MQ, _C_MK, _C_MGATE, _C_END = 1792, 2304, 2816, 3328


def _sigmoid(x):
    return 1.0 / (1.0 + jnp.exp(-x))


def _dot(a, b):
    return jnp.dot(a, b, preferred_element_type=F32)


def _dot_nt(a, b):
    return lax.dot_general(a, b, (((1,), (1,)), ((), ())), preferred_element_type=F32)


def _dot_tn(a, b):
    return lax.dot_general(a, b, (((0,), (0,)), ((), ())), preferred_element_type=F32)


def _ret_tables():
    h = np.arange(RET_HEADS, dtype=np.float64)
    log_g = np.log(1.0 - np.exp2(-5.0 - h))
    i = np.arange(RET_CHUNK, dtype=np.float64)
    diff = i[:, None] - i[None, :]
    dec = np.where(diff >= 0, np.exp(log_g[:, None, None] * np.maximum(diff, 0.0)), 0.0)
    scale = HEAD_DIM ** -0.5
    dec_stack = (dec * scale).reshape(RET_HEADS * RET_CHUNK, RET_CHUNK)
    zeta = np.exp(log_g[:, None] * (RET_CHUNK - 1 - i))
    xi = np.exp(log_g[:, None] * (i + 1.0))
    zeta_tab = np.repeat(zeta.T, HEAD_DIM, axis=1) * scale
    xi_tab = np.repeat(xi.T, HEAD_DIM, axis=1)
    g_chunk = np.exp(log_g * RET_CHUNK)
    head_of = np.arange(RET_WIDTH) // HEAD_DIM
    same = head_of[:, None] == head_of[None, :]
    gtab = np.where(same, g_chunk[head_of][:, None], 0.0)
    f32 = lambda a: jnp.asarray(a, F32)
    return (f32(dec_stack), f32(zeta_tab), f32(xi_tab), f32(gtab), f32(same),
            jnp.asarray(same / HEAD_DIM, BF16))


def _group_mean(z, avg):
    hi = z.astype(BF16)
    lo = (z - hi.astype(F32)).astype(BF16)
    return _dot(hi, avg) + _dot(lo, avg)


def _front_kernel(x_ref, g_ref, w_ref, wvt_ref, qtab_ref,
                  cw_ref, cb_ref, lg_ref, lb_ref, pw_ref, pb_ref,
                  dec_ref, zeta_ref, xi_ref, gtab_ref, bmask_ref, avg_ref,
                  ya_ref, yr_ref, sg_ref, qa_ref, ka_ref, vt_ref,
                  hn_ref, ext_ref, sh_ref, sw_ref, r_ref, state_ref, *, tm):
    first = pl.program_id(1) == 0
    x = x_ref[0]
    ms = jnp.mean(x * x, axis=-1, keepdims=True)
    hn_ref[...] = (x * lax.rsqrt(ms + RMS_EPS) * g_ref[...]).astype(BF16)

    def proj(lo, hi):
        return _dot(hn_ref[...], w_ref[:, lo:hi])

    @pl.when(first)
    def _():
        ext_ref[0:HALO, :] = jnp.zeros((HALO, CONV_WIDTH), F32)
        state_ref[...] = jnp.zeros_like(state_ref)

    @pl.when(jnp.logical_not(first))
    def _():
        ext_ref[0:HALO, :] = ext_ref[tm:tm + HALO, :]

    a = proj(_C_AVAL, _C_AGATE)
    ext_ref[HALO:, :] = (a[:, :CONV_WIDTH] * _sigmoid(a[:, CONV_WIDTH:])).astype(BF16).astype(F32)
    span = HALO + tm - SUBLANES
    for r in range(1, SUBLANES):
        sh_ref[r - 1] = ext_ref[r:r + span, :]
    r_ref[...] = proj(_C_RQ, _C_RGATE).astype(BF16)

    def conv_chunk(c):
        base = HALO - (CONV_K - 1)
        r0 = c * CONV_ROWS
        acc = jnp.zeros((CONV_ROWS, CONV_WIDTH), F32) + cb_ref[...]
        for k in range(CONV_K):
            r = (base + k) % SUBLANES
            a0 = base + k - r + r0
            win = ext_ref[a0:a0 + CONV_ROWS, :] if r == 0 else sh_ref[r - 1, a0:a0 + CONV_ROWS, :]
            acc = acc + win * cw_ref[k:k + 1, :]
        mu = jnp.mean(acc, axis=-1, keepdims=True)
        dlt = acc - mu
        var = jnp.mean(dlt * dlt, axis=-1, keepdims=True)
        yn = dlt * lax.rsqrt(var + LN_EPS) * lg_ref[...] + lb_ref[...]
        sw_ref[r0:r0 + CONV_ROWS, :] = (yn * _sigmoid(yn)).astype(BF16)

    def conv_out():
        ya_ref[0] = (_dot(sw_ref[...], pw_ref[...]) + pb_ref[...]).astype(BF16)

    c = RET_CHUNK
    lane_head = lax.broadcasted_iota(jnp.int32, (c, RET_WIDTH), 1) // HEAD_DIM

    def retention():
        chunks = range(tm // c)
        sls = [slice(ci * c, (ci + 1) * c) for ci in chunks]
        avg = avg_ref[...]
        probs, kzs = [], []
        for sl in sls:
            qf = r_ref[sl, 0:RET_WIDTH].astype(F32)
            k = r_ref[sl, RET_WIDTH:2 * RET_WIDTH]
            qs = jnp.concatenate([jnp.where(lane_head == h, qf, 0.0).astype(BF16) for h in range(RET_HEADS)],
                                 axis=0)
            probs.append((_dot_nt(qs, k) * dec_ref[...]).astype(BF16))
            kzs.append((k.astype(F32) * zeta_ref[...]).astype(BF16))
        yield
        intras, states = [], []
        st = state_ref[...]
        for sl, p, kz in zip(sls, probs, kzs):
            v = r_ref[sl, 2 * RET_WIDTH:3 * RET_WIDTH]
            full = _dot(p, v)
            intra = jnp.zeros((c, RET_WIDTH), F32)
            for h in range(RET_HEADS):
                intra = jnp.where(lane_head == h, full[h * c:(h + 1) * c, :], intra)
            intras.append(intra)
            states.append(st.astype(BF16))
            st = _dot_tn(kz, v) * bmask_ref[...] + st * gtab_ref[...]
        state_ref[...] = st
        yield
        outs, mus = [], []
        for sl, intra, stb in zip(sls, intras, states):
            o = intra + _dot(r_ref[sl, 0:RET_WIDTH], stb) * xi_ref[...]
            outs.append(o)
            mus.append(_group_mean(o, avg))
        yield
        for sl, o, mu in zip(sls, outs, mus):
            dlt = o - mu
            var = _group_mean(dlt * dlt, avg)
            yr_ref[0, sl, :] = (dlt * lax.rsqrt(var + LN_EPS)).astype(BF16)
        yield

    def silu(t):
        return t * _sigmoid(t)

    def gates():
        sg_ref[0, :, 0:256] = silu(proj(_C_AGATE, _C_RQ)).astype(BF16)
        sg_ref[0, :, 256:512] = silu(proj(_C_RGATE, _C_MQ)).astype(BF16)

    def gate_m(half):
        lo = half * 256
        sg_ref[0, :, 512 + lo:768 + lo] = silu(proj(_C_MGATE + lo, _C_MGATE + lo + 256)).astype(BF16)

    def moba_qk(half):
        lane = lax.broadcasted_iota(jnp.int32, (tm, LANES), 1)
        low = lane < HEAD_DIM
        row = lax.broadcasted_iota(jnp.int32, (tm, LANES), 0) + pl.program_id(1) * tm
        pos_lo = (row % MOBA_BLOCK).astype(F32)
        pos_hi = (row - row % MOBA_BLOCK).astype(F32)
        zero = jnp.zeros((tm, LANES), F32)
        kb_lowhead = jnp.where((lane >= HEAD_DIM) & (lane < HEAD_DIM + 2 * BIAS_TERMS),
                               jnp.where(lane % 2 == 0, pos_lo, pos_hi), zero)
        kb_highhead = jnp.where(lane < 2 * BIAS_TERMS, jnp.where(lane % 2 == 0, pos_lo, pos_hi), zero)
        lo = half * 256
        mq = proj(_C_MQ + lo, _C_MQ + lo + 256) * (HEAD_DIM ** -0.5 * LOG2E)
        mk = proj(_C_MK + lo, _C_MK + lo + 256)
        for gl in range(2):
            g = 2 * half + gl
            qg = mq[:, gl * LANES:(gl + 1) * LANES]
            kg = mk[:, gl * LANES:(gl + 1) * LANES]
            qa_ref[0, g] = jnp.where(low, qg, qtab_ref[g:g + 1, :]).astype(BF16)
            qa_ref[0, g + 4] = jnp.where(low, qtab_ref[g + 4:g + 5, :], qg).astype(BF16)
            ka_ref[0, g] = jnp.where(low, kg, kb_lowhead).astype(BF16)
            ka_ref[0, g + 4] = jnp.where(low, kb_highhead, kg).astype(BF16)

    def moba_v(half):
        vt = _dot_nt(wvt_ref[half * 256:(half + 1) * 256, :], hn_ref[...])
        sub = lax.broadcasted_iota(jnp.int32, (VT_ROWS - HEAD_DIM, MOBA_BLOCK), 0)
        ones_row = jnp.where(sub == 0, 1.0, 0.0).astype(BF16)
        for hl in range(ATT_HEADS // 2):
            h = half * (ATT_HEADS // 2) + hl
            for cb in range(tm // MOBA_BLOCK):
                vt_ref[0, h, cb, 0:HEAD_DIM, :] = vt[hl * HEAD_DIM:(hl + 1) * HEAD_DIM,
                                                     cb * MOBA_BLOCK:(cb + 1) * MOBA_BLOCK].astype(BF16)
                vt_ref[0, h, cb, HEAD_DIM:, :] = ones_row

    n_conv = tm // CONV_ROWS
    mxu_work = [gates] + [functools.partial(f, half) for f in (gate_m, moba_qk, moba_v) for half in range(2)]
    ret_stages = retention()
    n_ret_stages = 4
    n_slots = len(mxu_work)
    ret_done = 0
    for slot in range(n_slots):
        if ret_done < n_ret_stages and slot * n_ret_stages >= ret_done * n_slots:
            next(ret_stages)
            ret_done += 1
        for cc in range(slot * n_conv // n_slots, (slot + 1) * n_conv // n_slots):
            conv_chunk(cc)
        mxu_work[slot]()
    for _ in ret_stages:
        pass
    conv_out()


def _front(x, g, w, wvt, qtab, conv_params, tables, *, tm):
    b, s, d = x.shape
    row_spec = lambda c: pl.BlockSpec((1, tm, c), lambda i, j: (i, j, 0))
    full2 = lambda a: pl.BlockSpec(a.shape, lambda i, j: (0, 0))
    consts = (g, w, wvt, qtab) + tuple(conv_params) + tuple(tables)
    out_shape = (
        jax.ShapeDtypeStruct((b, s, CONV_WIDTH), BF16),
        jax.ShapeDtypeStruct((b, s, RET_WIDTH), BF16),
        jax.ShapeDtypeStruct((b, s, D_MODEL), BF16),
        jax.ShapeDtypeStruct((b, ATT_HEADS, s, LANES), BF16),
        jax.ShapeDtypeStruct((b, ATT_HEADS, s, LANES), BF16),
        jax.ShapeDtypeStruct((b, ATT_HEADS, s // MOBA_BLOCK, VT_ROWS, MOBA_BLOCK), BF16),
    )
    out_specs = (
        row_spec(CONV_WIDTH), row_spec(RET_WIDTH), row_spec(D_MODEL),
        pl.BlockSpec((1, ATT_HEADS, tm, LANES), lambda i, j: (i, 0, j, 0)),
        pl.BlockSpec((1, ATT_HEADS, tm, LANES), lambda i, j: (i, 0, j, 0)),
        pl.BlockSpec((1, ATT_HEADS, tm // MOBA_BLOCK, VT_ROWS, MOBA_BLOCK), lambda i, j: (i, 0, j, 0, 0)),
    )
    return pl.pallas_call(
        functools.partial(_front_kernel, tm=tm),
        grid=(b, s // tm),
        in_specs=[row_spec(d)] + [full2(t) for t in consts],
        out_specs=out_specs,
        out_shape=out_shape,
        scratch_shapes=[
            pltpu.VMEM((tm, d), BF16),
            pltpu.VMEM((HALO + tm, CONV_WIDTH), F32),
            pltpu.VMEM((SUBLANES - 1, HALO + tm - SUBLANES, CONV_WIDTH), F32),
            pltpu.VMEM((tm, CONV_WIDTH), BF16),
            pltpu.VMEM((tm, 3 * RET_WIDTH), BF16),
            pltpu.VMEM((RET_WIDTH, RET_WIDTH), F32),
        ],
        compiler_params=pltpu.CompilerParams(
            dimension_semantics=("parallel", "arbitrary"), vmem_limit_bytes=VMEM_LIMIT),
        name="front",
    )(x, *consts)


def _moba_kernel(q_ref, k_ref, vt_ref, y_ref, km_ref, s_ref, m_ref, mb_ref, *, nblk, qt):
    head = pl.program_id(1)
    t = pl.program_id(2)
    blk = MOBA_BLOCK
    nq = qt * blk
    lane = lax.broadcasted_iota(jnp.int32, (1, LANES), 1)
    real_lanes = (lane < HEAD_DIM) == (head < ATT_HEADS // 2)

    @pl.when(t == 0)
    def _():
        rows = []
        for i in range(nblk):
            kb = k_ref[0, 0, i * blk:(i + 1) * blk, :].astype(F32)
            rows.append(jnp.sum(kb, axis=0, keepdims=True) * (1.0 / blk))
        km = jnp.where(real_lanes, jnp.concatenate(rows, axis=0), 0.0)
        hi = km.astype(BF16)
        km_ref[0:nblk, :] = hi
        km_ref[nblk:2 * nblk, :] = (km - hi.astype(F32)).astype(BF16)

    blk_id = lax.broadcasted_iota(jnp.int32, (nblk, nq), 0)
    q_blk = lax.broadcasted_iota(jnp.int32, (nblk, nq), 1) // blk
    kdiff = (lax.broadcasted_iota(jnp.int32, (blk, blk), 0)
             - lax.broadcasted_iota(jnp.int32, (blk, blk), 1))

    def logits_stage(tt, slot):
        j0 = tt * qt
        q_all = q_ref[0, 0]
        g2 = _dot_nt(km_ref[...], q_all)
        gate = g2[0:nblk] + g2[nblk:2 * nblk]
        past = blk_id < q_blk + j0
        gm = jnp.where(past, gate, NEG)
        cnt = jnp.zeros((nblk, nq), jnp.int32)
        for i2 in range(nblk):
            r = gm[i2:i2 + 1, :]
            beats = (r > gm) | ((r == gm) & (blk_id > i2))
            cnt = cnt + beats.astype(jnp.int32)
        sel = ((cnt < MOBA_TOPK) & past) | (blk_id == q_blk + j0)
        mb_all = jnp.where(sel, 0.0, NEG)
        mb_ref[slot] = mb_all

        off = 0
        for jl in range(qt):
            j = j0 + jl
            q = q_all[jl * blk:(jl + 1) * blk, :]
            mb = mb_all[:, jl * blk:(jl + 1) * blk]
            m8 = jnp.full((8, blk), NEG, F32)
            for i in range(j + 1):
                s = _dot_nt(k_ref[0, 0, i * blk:(i + 1) * blk, :], q)
                if i == j:
                    s = jnp.where(kdiff > 0, NEG, s)
                s_ref[slot, off + i] = s
                m8 = jnp.maximum(m8, jnp.max(s.reshape(blk // 8, 8, blk), axis=0) + mb[i:i + 1, :])
                if i == j:
                    m_ref[slot, jl] = jnp.max(m8, axis=0, keepdims=True)
                yield
            off += j + 1

    def value_stage(tt, slot):
        j0 = tt * qt
        off = 0
        for jl in range(qt):
            j = j0 + jl
            m = m_ref[slot, jl]
            acc = jnp.zeros((VT_ROWS, blk), F32)
            for i in range(j + 1):
                shift = m - mb_ref[slot, i:i + 1, jl * blk:(jl + 1) * blk]
                p = jnp.exp2(s_ref[slot, off + i] - shift).astype(BF16)
                acc = acc + _dot(vt_ref[0, 0, i], p)
                if i == j:
                    o = acc[0:HEAD_DIM, :] * (1.0 / acc[HEAD_DIM:HEAD_DIM + 1, :])
                    y_ref[0, 0, :, jl * blk:(jl + 1) * blk] = o.astype(BF16)
                yield
            off += j + 1

    n_tiles = nblk // qt

    def n_pairs(tt):
        return sum(tt * qt + jl + 1 for jl in range(qt))

    def step(st):
        streams = []
        if st < n_tiles:
            streams.append((logits_stage(n_tiles - 1 - st, st % 2), n_pairs(n_tiles - 1 - st)))
        if st >= 1:
            streams.append((value_stage(n_tiles - st, (st - 1) % 2), n_pairs(n_tiles - st)))
        done = [0] * len(streams)
        while any(d < n for d, (_, n) in zip(done, streams)):
            k = min((d / n, idx) for idx, (d, (_, n)) in enumerate(zip(done, streams)) if d < n)[1]
            next(streams[k][0])
            done[k] += 1
        for gen, _ in streams:
            for _ in gen:
                pass

    for st in range(n_tiles + 1):
        pl.when(t == st)(functools.partial(step, st))


def _moba(qa, ka, vt):
    b, nh, s, _ = qa.shape
    nblk = s // MOBA_BLOCK
    qt = min(4, nblk)
    n_tiles = nblk // qt
    n_logit_blocks = qt * nblk - qt * (qt - 1) // 2
    return pl.pallas_call(
        functools.partial(_moba_kernel, nblk=nblk, qt=qt),
        grid=(b, nh, n_tiles + 1),
        in_specs=[
            pl.BlockSpec((1, 1, qt * MOBA_BLOCK, LANES), lambda i, h, t: (i, h, jnp.maximum(n_tiles - 1 - t, 0), 0)),
            pl.BlockSpec((1, 1, s, LANES), lambda i, h, t: (i, h, 0, 0)),
            pl.BlockSpec((1, 1, nblk, VT_ROWS, MOBA_BLOCK), lambda i, h, t: (i, h, 0, 0, 0)),
        ],
        out_specs=pl.BlockSpec((1, 1, HEAD_DIM, qt * MOBA_BLOCK),
                               lambda i, h, t: (i, h, 0, jnp.minimum(n_tiles - t, n_tiles - 1))),
        out_shape=jax.ShapeDtypeStruct((b, nh, HEAD_DIM, s), BF16),
        scratch_shapes=[
            pltpu.VMEM((2 * nblk, LANES), BF16),
            pltpu.VMEM((2, n_logit_blocks, MOBA_BLOCK, MOBA_BLOCK), F32),
            pltpu.VMEM((2, qt, 1, MOBA_BLOCK), F32),
            pltpu.VMEM((2, nblk, qt * MOBA_BLOCK), F32),
        ],
        compiler_params=pltpu.CompilerParams(
            dimension_semantics=("parallel", "parallel", "arbitrary"), vmem_limit_bytes=VMEM_LIMIT),
        name="moba",
    )(qa, ka, vt)


def _outproj_kernel(ya_ref, yr_ref, ymt_ref, sg_ref, x_ref, w_ref, fg_ref, o_ref, *, final):
    tm = x_ref.shape[1]
    ym = ymt_ref[0].reshape(ATT_WIDTH, tm).astype(F32).T
    mix = jnp.concatenate([ya_ref[0].astype(F32), yr_ref[0].astype(F32), ym], axis=-1)
    mix = (mix * sg_ref[0].astype(F32)).astype(BF16)
    xn = x_ref[0] + _dot(mix, w_ref[...])
    if final:
        ms = jnp.mean(xn * xn, axis=-1, keepdims=True)
        xn = xn * lax.rsqrt(ms + RMS_EPS) * fg_ref[...]
    o_ref[0] = xn


def _outproj(ya, yr, ym, sg, x, w, fg, *, tm, final):
    b, s, d = x.shape
    row_spec = lambda c: pl.BlockSpec((1, tm, c), lambda i, j: (i, j, 0))
    full2 = lambda a: pl.BlockSpec(a.shape, lambda i, j: (0, 0))
    return pl.pallas_call(
        functools.partial(_outproj_kernel, final=final),
        grid=(b, s // tm),
        in_specs=[row_spec(CONV_WIDTH), row_spec(RET_WIDTH),
                  pl.BlockSpec((1, ATT_HEADS, HEAD_DIM, tm), lambda i, j: (i, 0, 0, j)),
                  row_spec(d), row_spec(d), full2(w), full2(fg)],
        out_specs=row_spec(d),
        out_shape=jax.ShapeDtypeStruct((b, s, d), F32),
        compiler_params=pltpu.CompilerParams(
            dimension_semantics=("parallel", "parallel"), vmem_limit_bytes=VMEM_LIMIT),
        name="outproj",
    )(ya, yr, ym, sg, x, w, fg)


def _pack_weights(w_in_l):
    sizes = (CONV_WIDTH,) * 3 + (RET_WIDTH,) * 4 + (ATT_WIDTH,) * 4
    pts = np.cumsum(sizes)[:-1]
    a_val, a_glu, a_gate, r_q, r_k, r_v, r_gate, m_q, m_k, m_v, m_gate = jnp.split(w_in_l, pts, axis=-1)
    perm = np.concatenate([np.r_[g * HEAD_DIM:(g + 1) * HEAD_DIM, (g + 4) * HEAD_DIM:(g + 5) * HEAD_DIM]
                           for g in range(ATT_HEADS // 2)])
    w = jnp.concatenate([a_val, a_glu, a_gate, r_q, r_k, r_v, r_gate, m_q[:, perm], m_k[:, perm], m_gate],
                        axis=-1).astype(BF16)
    return w, m_v.T.astype(BF16)


def _bf16_round(x):
    bits = np.asarray(x, np.float32).view(np.uint32).astype(np.uint64)
    bits = (bits + 0x7FFF + ((bits >> 16) & 1)) & 0xFFFF0000
    return bits.astype(np.uint32).view(np.float32)


def _q_bias_table():
    slopes = np.exp2(-8.0 * (np.arange(ATT_HEADS, dtype=np.float64) + 1.0) / ATT_HEADS)
    tab = np.zeros((ATT_HEADS, LANES), np.float32)
    for h in range(ATT_HEADS):
        base = HEAD_DIM if h < ATT_HEADS // 2 else 0
        rest = slopes[h] * LOG2E
        for t in range(BIAS_TERMS):
            term = float(_bf16_round(rest))
            tab[h, base + 2 * t] = term
            tab[h, base + 2 * t + 1] = term
            rest -= term
    return jnp.asarray(tab)


def kernel(x, norm_g, w_in, conv_w, conv_b, conv_ln_g, conv_ln_b, conv_pw_w, conv_pw_b, w_out, final_g):
    b, s, d = x.shape
    tm = min(512, s)
    qtab = _q_bias_table()
    tables = _ret_tables()
    fg = final_g.reshape(1, d)
    for layer in range(DEPTH):
        w, wvt = _pack_weights(w_in[layer])
        conv_params = (conv_w[layer], conv_b[layer].reshape(1, -1), conv_ln_g[layer].reshape(1, -1),
                       conv_ln_b[layer].reshape(1, -1), conv_pw_w[layer].astype(BF16),
                       conv_pw_b[layer].reshape(1, -1))
        ya, yr, sg, qa, ka, vt = _front(x, norm_g[layer].reshape(1, d), w, wvt, qtab, conv_params, tables, tm=tm)
        ym = _moba(qa, ka, vt)
        x = _outproj(ya, yr, ym, sg, x, w_out[layer].astype(BF16), fg, tm=min(2 * tm, s),
                     final=(layer == DEPTH - 1))
    return x
```

```python
import functools

import numpy as np
import jax
import jax.numpy as jnp
from jax import lax
from jax.experimental import pallas as pl
from jax.experimental.pallas import tpu as pltpu

D_MODEL = 1024
DEPTH = 2
HEAD_DIM = 64
CONV_WIDTH = 256
RET_WIDTH = 256
ATT_WIDTH = 512
RET_HEADS = RET_WIDTH // HEAD_DIM
ATT_HEADS = ATT_WIDTH // HEAD_DIM
CONV_K = 31
RET_CHUNK = 128
MOBA_BLOCK = 256
MOBA_TOPK = 3
RMS_EPS = 1e-6
LN_EPS = 1e-5
NEG = -1e30
LOG2E = 1.4426950408889634
BIAS_TERMS = 3

LANES = 128
SUBLANES = 8
VT_ROWS = HEAD_DIM + 16
CONV_ROWS = 64
HALO = 32
VMEM_LIMIT = 56 * 1024 * 1024

F32 = jnp.float32
BF16 = jnp.bfloat16

_C_AVAL, _C_AGLU, _C_AGATE = 0, 256, 512
_C_RQ, _C_RK, _C_RV, _C_RGATE = 768, 1024, 1280, 1536
_C_MQ, _C_MK, _C_MGATE, _C_END = 1792, 2304, 2816, 3328


def _sigmoid(x):
    return 1.0 / (1.0 + jnp.exp(-x))


def _dot(a, b):
    return jnp.dot(a, b, preferred_element_type=F32)


def _dot_nt(a, b):
    return lax.dot_general(a, b, (((1,), (1,)), ((), ())), preferred_element_type=F32)


def _dot_tn(a, b):
    return lax.dot_general(a, b, (((0,), (0,)), ((), ())), preferred_element_type=F32)


def _ret_tables():
    h = np.arange(RET_HEADS, dtype=np.float64)
    log_g = np.log(1.0 - np.exp2(-5.0 - h))
    i = np.arange(RET_CHUNK, dtype=np.float64)
    diff = i[:, None] - i[None, :]
    dec = np.where(diff >= 0, np.exp(log_g[:, None, None] * np.maximum(diff, 0.0)), 0.0)
    scale = HEAD_DIM ** -0.5
    dec_stack = (dec * scale).reshape(RET_HEADS * RET_CHUNK, RET_CHUNK)
    zeta = np.exp(log_g[:, None] * (RET_CHUNK - 1 - i))
    xi = np.exp(log_g[:, None] * (i + 1.0))
    zeta_tab = np.repeat(zeta.T, HEAD_DIM, axis=1) * scale
    xi_tab = np.repeat(xi.T, HEAD_DIM, axis=1)
    g_chunk = np.exp(log_g * RET_CHUNK)
    head_of = np.arange(RET_WIDTH) // HEAD_DIM
    same = head_of[:, None] == head_of[None, :]
    gtab = np.where(same, g_chunk[head_of][:, None], 0.0)
    f32 = lambda a: jnp.asarray(a, F32)
    return (f32(dec_stack), f32(zeta_tab), f32(xi_tab), f32(gtab), f32(same),
            jnp.asarray(same / HEAD_DIM, BF16))


def _group_mean(z, avg):
    hi = z.astype(BF16)
    lo = (z - hi.astype(F32)).astype(BF16)
    return _dot(hi, avg) + _dot(lo, avg)


def _front_kernel(x_ref, g_ref, w_ref, wvt_ref, qtab_ref,
                  cw_ref, cb_ref, lg_ref, lb_ref, pw_ref, pb_ref,
                  dec_ref, zeta_ref, xi_ref, gtab_ref, bmask_ref, avg_ref,
                  ya_ref, yr_ref, sg_ref, qa_ref, ka_ref, vt_ref,
                  hn_ref, ext_ref, sh_ref, sw_ref, r_ref, state_ref, *, tm):
    first = pl.program_id(1) == 0
    x = x_ref[0]
    ms = jnp.mean(x * x, axis=-1, keepdims=True)
    hn_ref[...] = (x * lax.rsqrt(ms + RMS_EPS) * g_ref[...]).astype(BF16)

    def proj(lo, hi):
        return _dot(hn_ref[...], w_ref[:, lo:hi])

    @pl.when(first)
    def _():
        ext_ref[0:HALO, :] = jnp.zeros((HALO, CONV_WIDTH), F32)
        state_ref[...] = jnp.zeros_like(state_ref)

    @pl.when(jnp.logical_not(first))
    def _():
        ext_ref[0:HALO, :] = ext_ref[tm:tm + HALO, :]

    a = proj(_C_AVAL, _C_AGATE)
    ext_ref[HALO:, :] = (a[:, :CONV_WIDTH] * _sigmoid(a[:, CONV_WIDTH:])).astype(BF16).astype(F32)
    span = HALO + tm - SUBLANES
    for r in range(1, SUBLANES):
        sh_ref[r - 1] = ext_ref[r:r + span, :]
    r_ref[...] = proj(_C_RQ, _C_RGATE).astype(BF16)

    def conv_chunk(c):
        base = HALO - (CONV_K - 1)
        r0 = c * CONV_ROWS
        acc = jnp.zeros((CONV_ROWS, CONV_WIDTH), F32) + cb_ref[...]
        for k in range(CONV_K):
            r = (base + k) % SUBLANES
            a0 = base + k - r + r0
            win = ext_ref[a0:a0 + CONV_ROWS, :] if r == 0 else sh_ref[r - 1, a0:a0 + CONV_ROWS, :]
            acc = acc + win * cw_ref[k:k + 1, :]
        mu = jnp.mean(acc, axis=-1, keepdims=True)
        dlt = acc - mu
        var = jnp.mean(dlt * dlt, axis=-1, keepdims=True)
        yn = dlt * lax.rsqrt(var + LN_EPS) * lg_ref[...] + lb_ref[...]
        sw_ref[r0:r0 + CONV_ROWS, :] = (yn * _sigmoid(yn)).astype(BF16)

    def conv_out():
        ya_ref[0] = (_dot(sw_ref[...], pw_ref[...]) + pb_ref[...]).astype(BF16)

    c = RET_CHUNK
    lane_head = lax.broadcasted_iota(jnp.int32, (c, RET_WIDTH), 1) // HEAD_DIM

    def retention():
        chunks = range(tm // c)
        sls = [slice(ci * c, (ci + 1) * c) for ci in chunks]
        avg = avg_ref[...]
        probs, kzs = [], []
        for sl in sls:
            qf = r_ref[sl, 0:RET_WIDTH].astype(F32)
            k = r_ref[sl, RET_WIDTH:2 * RET_WIDTH]
            qs = jnp.concatenate([jnp.where(lane_head == h, qf, 0.0).astype(BF16) for h in range(RET_HEADS)],
                                 axis=0)
            probs.append((_dot_nt(qs, k) * dec_ref[...]).astype(BF16))
            kzs.append((k.astype(F32) * zeta_ref[...]).astype(BF16))
        yield
        intras, states = [], []
        st = state_ref[...]
        for sl, p, kz in zip(sls, probs, kzs):
            v = r_ref[sl, 2 * RET_WIDTH:3 * RET_WIDTH]
            full = _dot(p, v)
            intra = jnp.zeros((c, RET_WIDTH), F32)
            for h in range(RET_HEADS):
                intra = jnp.where(lane_head == h, full[h * c:(h + 1) * c, :], intra)
            intras.append(intra)
            states.append(st.astype(BF16))
            st = _dot_tn(kz, v) * bmask_ref[...] + st * gtab_ref[...]
        state_ref[...] = st
        yield
        outs, mus = [], []
        for sl, intra, stb in zip(sls, intras, states):
            o = intra + _dot(r_ref[sl, 0:RET_WIDTH], stb) * xi_ref[...]
            outs.append(o)
            mus.append(_group_mean(o, avg))
        yield
        for sl, o, mu in zip(sls, outs, mus):
            dlt = o - mu
            var = _group_mean(dlt * dlt, avg)
            yr_ref[0, sl, :] = (dlt * lax.rsqrt(var + LN_EPS)).astype(BF16)
        yield

    def silu(t):
        return t * _sigmoid(t)

    def gates():
        sg_ref[0, :, 0:256] = silu(proj(_C_AGATE, _C_RQ)).astype(BF16)
        sg_ref[0, :, 256:512] = silu(proj(_C_RGATE, _C_MQ)).astype(BF16)

    def gate_m(half):
        lo = half * 256
        sg_ref[0, :, 512 + lo:768 + lo] = silu(proj(_C_MGATE + lo, _C_MGATE + lo + 256)).astype(BF16)

    def moba_qk(half):
        lane = lax.broadcasted_iota(jnp.int32, (tm, LANES), 1)
        low = lane < HEAD_DIM
        row = lax.broadcasted_iota(jnp.int32, (tm, LANES), 0) + pl.program_id(1) * tm
        pos_lo = (row % MOBA_BLOCK).astype(F32)
        pos_hi = (row - row % MOBA_BLOCK).astype(F32)
        zero = jnp.zeros((tm, LANES), F32)
        kb_lowhead = jnp.where((lane >= HEAD_DIM) & (lane < HEAD_DIM + 2 * BIAS_TERMS),
                               jnp.where(lane % 2 == 0, pos_lo, pos_hi), zero)
        kb_highhead = jnp.where(lane < 2 * BIAS_TERMS, jnp.where(lane % 2 == 0, pos_lo, pos_hi), zero)
        lo = half * 256
        mq = proj(_C_MQ + lo, _C_MQ + lo + 256) * (HEAD_DIM ** -0.5 * LOG2E)
        mk = proj(_C_MK + lo, _C_MK + lo + 256)
        for gl in range(2):
            g = 2 * half + gl
            qg = mq[:, gl * LANES:(gl + 1) * LANES]
            kg = mk[:, gl * LANES:(gl + 1) * LANES]
            qa_ref[0, g] = jnp.where(low, qg, qtab_ref[g:g + 1, :]).astype(BF16)
            qa_ref[0, g + 4] = jnp.where(low, qtab_ref[g + 4:g + 5, :], qg).astype(BF16)
            ka_ref[0, g] = jnp.where(low, kg, kb_lowhead).astype(BF16)
            ka_ref[0, g + 4] = jnp.where(low, kb_highhead, kg).astype(BF16)

    def moba_v(half):
        vt = _dot_nt(wvt_ref[half * 256:(half + 1) * 256, :], hn_ref[...])
        sub = lax.broadcasted_iota(jnp.int32, (VT_ROWS - HEAD_DIM, MOBA_BLOCK), 0)
        ones_row = jnp.where(sub == 0, 1.0, 0.0).astype(BF16)
        for hl in range(ATT_HEADS // 2):
            h = half * (ATT_HEADS // 2) + hl
            for cb in range(tm // MOBA_BLOCK):
                vt_ref[0, h, cb, 0:HEAD_DIM, :] = vt[hl * HEAD_DIM:(hl + 1) * HEAD_DIM,
                                                     cb * MOBA_BLOCK:(cb + 1) * MOBA_BLOCK].astype(BF16)
                vt_ref[0, h, cb, HEAD_DIM:, :] = ones_row

    n_conv = tm // CONV_ROWS
    mxu_work = [gates] + [functools.partial(f, half) for f in (gate_m, moba_qk, moba_v) for half in range(2)]
    ret_stages = retention()
    n_ret_stages = 4
    n_slots = len(mxu_work)
    ret_done = 0
    for slot in range(n_slots):
        if ret_done < n_ret_stages and slot * n_ret_stages >= ret_done * n_slots:
            next(ret_stages)
            ret_done += 1
        for cc in range(slot * n_conv // n_slots, (slot + 1) * n_conv // n_slots):
            conv_chunk(cc)
        mxu_work[slot]()
    for _ in ret_stages:
        pass
    conv_out()


def _front(x, g, w, wvt, qtab, conv_params, tables, *, tm):
    b, s, d = x.shape
    row_spec = lambda c: pl.BlockSpec((1, tm, c), lambda i, j: (i, j, 0))
    full2 = lambda a: pl.BlockSpec(a.shape, lambda i, j: (0, 0))
    consts = (g, w, wvt, qtab) + tuple(conv_params) + tuple(tables)
    out_shape = (
        jax.ShapeDtypeStruct((b, s, CONV_WIDTH), BF16),
        jax.ShapeDtypeStruct((b, s, RET_WIDTH), BF16),
        jax.ShapeDtypeStruct((b, s, D_MODEL), BF16),
        jax.ShapeDtypeStruct((b, ATT_HEADS, s, LANES), BF16),
        jax.ShapeDtypeStruct((b, ATT_HEADS, s, LANES), BF16),
        jax.ShapeDtypeStruct((b, ATT_HEADS, s // MOBA_BLOCK, VT_ROWS, MOBA_BLOCK), BF16),
    )
    out_specs = (
        row_spec(CONV_WIDTH), row_spec(RET_WIDTH), row_spec(D_MODEL),
        pl.BlockSpec((1, ATT_HEADS, tm, LANES), lambda i, j: (i, 0, j, 0)),
        pl.BlockSpec((1, ATT_HEADS, tm, LANES), lambda i, j: (i, 0, j, 0)),
        pl.BlockSpec((1, ATT_HEADS, tm // MOBA_BLOCK, VT_ROWS, MOBA_BLOCK), lambda i, j: (i, 0, j, 0, 0)),
    )
    return pl.pallas_call(
        functools.partial(_front_kernel, tm=tm),
        grid=(b, s // tm),
        in_specs=[row_spec(d)] + [full2(t) for t in consts],
        out_specs=out_specs,
        out_shape=out_shape,
        scratch_shapes=[
            pltpu.VMEM((tm, d), BF16),
            pltpu.VMEM((HALO + tm, CONV_WIDTH), F32),
            pltpu.VMEM((SUBLANES - 1, HALO + tm - SUBLANES, CONV_WIDTH), F32),
            pltpu.VMEM((tm, CONV_WIDTH), BF16),
            pltpu.VMEM((tm, 3 * RET_WIDTH), BF16),
            pltpu.VMEM((RET_WIDTH, RET_WIDTH), F32),
        ],
        compiler_params=pltpu.CompilerParams(
            dimension_semantics=("parallel", "arbitrary"), vmem_limit_bytes=VMEM_LIMIT),
        name="front",
    )(x, *consts)


def _moba_kernel(q_ref, k_ref, vt_ref, y_ref, km_ref, s_ref, m_ref, mb_ref, *, nblk, qt):
    n_tiles = nblk // qt
    g = pl.program_id(1)
    last = ATT_HEADS * n_tiles
    head = jnp.minimum(g // n_tiles, ATT_HEADS - 1)
    blk = MOBA_BLOCK
    nq = qt * blk
    lane = lax.broadcasted_iota(jnp.int32, (1, LANES), 1)
    real_lanes = (lane < HEAD_DIM) == (head < ATT_HEADS // 2)

    @pl.when((g % n_tiles == 0) & (g < last))
    def _():
        rows = []
        for i in range(nblk):
            kb = k_ref[0, 0, i * blk:(i + 1) * blk, :].astype(F32)
            rows.append(jnp.sum(kb, axis=0, keepdims=True) * (1.0 / blk))
        km = jnp.where(real_lanes, jnp.concatenate(rows, axis=0), 0.0)
        hi = km.astype(BF16)
        km_ref[0:nblk, :] = hi
        km_ref[nblk:2 * nblk, :] = (km - hi.astype(F32)).astype(BF16)

    blk_id = lax.broadcasted_iota(jnp.int32, (nblk, nq), 0)
    q_blk = lax.broadcasted_iota(jnp.int32, (nblk, nq), 1) // blk
    kdiff = (lax.broadcasted_iota(jnp.int32, (blk, blk), 0)
             - lax.broadcasted_iota(jnp.int32, (blk, blk), 1))

    def logits_stage(tt, slot):
        j0 = tt * qt
        q_all = q_ref[0, 0]
        g2 = _dot_nt(km_ref[...], q_all)
        gate = g2[0:nblk] + g2[nblk:2 * nblk]
        past = blk_id < q_blk + j0
        gm = jnp.where(past, gate, NEG)
        cnt = jnp.zeros((nblk, nq), jnp.int32)
        for i2 in range(nblk):
            r = gm[i2:i2 + 1, :]
            beats = (r > gm) | ((r == gm) & (blk_id > i2))
            cnt = cnt + beats.astype(jnp.int32)
        sel = ((cnt < MOBA_TOPK) & past) | (blk_id == q_blk + j0)
        mb_all = jnp.where(sel, 0.0, NEG)
        mb_ref[slot] = mb_all

        off = 0
        for jl in range(qt):
            j = j0 + jl
            q = q_all[jl * blk:(jl + 1) * blk, :]
            mb = mb_all[:, jl * blk:(jl + 1) * blk]
            m8 = jnp.full((8, blk), NEG, F32)
            for i in range(j + 1):
                s = _dot_nt(k_ref[0, 0, i * blk:(i + 1) * blk, :], q)
                if i == j:
                    s = jnp.where(kdiff > 0, NEG, s)
                s_ref[slot, off + i] = s
                m8 = jnp.maximum(m8, jnp.max(s.reshape(blk // 8, 8, blk), axis=0) + mb[i:i + 1, :])
                if i == j:
                    m_ref[slot, jl] = jnp.max(m8, axis=0, keepdims=True)
                yield
            off += j + 1

    def value_stage(tt, slot):
        j0 = tt * qt
        off = 0
        for jl in range(qt):
            j = j0 + jl
            m = m_ref[slot, jl]
            acc = jnp.zeros((VT_ROWS, blk), F32)
            for i in range(j + 1):
                shift = m - mb_ref[slot, i:i + 1, jl * blk:(jl + 1) * blk]
                p = jnp.exp2(s_ref[slot, off + i] - shift).astype(BF16)
                acc = acc + _dot(vt_ref[0, 0, i], p)
                if i == j:
                    o = acc[0:HEAD_DIM, :] * (1.0 / acc[HEAD_DIM:HEAD_DIM + 1, :])
                    y_ref[0, 0, :, jl * blk:(jl + 1) * blk] = o.astype(BF16)
                yield
            off += j + 1

    n_tiles = nblk // qt

    def n_pairs(tt):
        return sum(tt * qt + jl + 1 for jl in range(qt))

    def step(logits_tile, value_tile):
        streams = []
        if logits_tile is not None:
            slot = (n_tiles - 1 - logits_tile) % 2
            streams.append((logits_stage(logits_tile, slot), n_pairs(logits_tile)))
        if value_tile is not None:
            slot = (n_tiles - 1 - value_tile) % 2
            streams.append((value_stage(value_tile, slot), n_pairs(value_tile)))
        done = [0] * len(streams)
        while any(d < n for d, (_, n) in zip(done, streams)):
            k = min((d / n, idx) for idx, (d, (_, n)) in enumerate(zip(done, streams)) if d < n)[1]
            next(streams[k][0])
            done[k] += 1
        for gen, _ in streams:
            for _ in gen:
                pass

    pl.when(g == 0)(functools.partial(step, n_tiles - 1, None))
    pl.when(g == last)(functools.partial(step, None, 0))
    for r in range(n_tiles):
        pl.when((g > 0) & (g < last) & (g % n_tiles == r))(
            functools.partial(step, n_tiles - 1 - r, (n_tiles - r) % n_tiles))


def _moba(qa, ka, vt):
    b, nh, s, _ = qa.shape
    nblk = s // MOBA_BLOCK
    qt = min(4, nblk)
    n_tiles = nblk // qt
    n_logit_blocks = qt * nblk - qt * (qt - 1) // 2
    assert nh == ATT_HEADS and n_tiles % 2 == 0, (nh, n_tiles)
    last = nh * n_tiles

    def logits_head(g):
        return jnp.minimum(g // n_tiles, nh - 1)

    def logits_tile(g):
        return jnp.where(g < last, n_tiles - 1 - g % n_tiles, 0)

    def value_step(g):
        return jnp.maximum(g - 1, 0)

    return pl.pallas_call(
        functools.partial(_moba_kernel, nblk=nblk, qt=qt),
        grid=(b, last + 1),
        in_specs=[
            pl.BlockSpec((1, 1, qt * MOBA_BLOCK, LANES), lambda i, g: (i, logits_head(g), logits_tile(g), 0)),
            pl.BlockSpec((1, 1, s, LANES), lambda i, g: (i, logits_head(g), 0, 0)),
            pl.BlockSpec((1, 1, nblk, VT_ROWS, MOBA_BLOCK),
                         lambda i, g: (i, value_step(g) // n_tiles, 0, 0, 0)),
        ],
        out_specs=pl.BlockSpec((1, 1, HEAD_DIM, qt * MOBA_BLOCK),
                               lambda i, g: (i, value_step(g) // n_tiles, 0,
                                             n_tiles - 1 - value_step(g) % n_tiles)),
        out_shape=jax.ShapeDtypeStruct((b, nh, HEAD_DIM, s), BF16),
        scratch_shapes=[
            pltpu.VMEM((2 * nblk, LANES), BF16),
            pltpu.VMEM((2, n_logit_blocks, MOBA_BLOCK, MOBA_BLOCK), F32),
            pltpu.VMEM((2, qt, 1, MOBA_BLOCK), F32),
            pltpu.VMEM((2, nblk, qt * MOBA_BLOCK), F32),
        ],
        compiler_params=pltpu.CompilerParams(
            dimension_semantics=("parallel", "arbitrary"), vmem_limit_bytes=VMEM_LIMIT),
        name="moba",
    )(qa, ka, vt)


def _outproj_kernel(ya_ref, yr_ref, ymt_ref, sg_ref, x_ref, w_ref, fg_ref, o_ref, *, final):
    tm = x_ref.shape[1]
    ym = ymt_ref[0].reshape(ATT_WIDTH, tm).astype(F32).T
    mix = jnp.concatenate([ya_ref[0].astype(F32), yr_ref[0].astype(F32), ym], axis=-1)
    mix = (mix * sg_ref[0].astype(F32)).astype(BF16)
    xn = x_ref[0] + _dot(mix, w_ref[...])
    if final:
        ms = jnp.mean(xn * xn, axis=-1, keepdims=True)
        xn = xn * lax.rsqrt(ms + RMS_EPS) * fg_ref[...]
    o_ref[0] = xn


def _outproj(ya, yr, ym, sg, x, w, fg, *, tm, final):
    b, s, d = x.shape
    row_spec = lambda c: pl.BlockSpec((1, tm, c), lambda i, j: (i, j, 0))
    full2 = lambda a: pl.BlockSpec(a.shape, lambda i, j: (0, 0))
    return pl.pallas_call(
        functools.partial(_outproj_kernel, final=final),
        grid=(b, s // tm),
        in_specs=[row_spec(CONV_WIDTH), row_spec(RET_WIDTH),
                  pl.BlockSpec((1, ATT_HEADS, HEAD_DIM, tm), lambda i, j: (i, 0, 0, j)),
                  row_spec(d), row_spec(d), full2(w), full2(fg)],
        out_specs=row_spec(d),
        out_shape=jax.ShapeDtypeStruct((b, s, d), F32),
        compiler_params=pltpu.CompilerParams(
            dimension_semantics=("parallel", "parallel"), vmem_limit_bytes=VMEM_LIMIT),
        name="outproj",
    )(ya, yr, ym, sg, x, w, fg)


def _pack_weights(w_in_l):
    sizes = (CONV_WIDTH,) * 3 + (RET_WIDTH,) * 4 + (ATT_WIDTH,) * 4
    pts = np.cumsum(sizes)[:-1]
    a_val, a_glu, a_gate, r_q, r_k, r_v, r_gate, m_q, m_k, m_v, m_gate = jnp.split(w_in_l, pts, axis=-1)
    perm = np.concatenate([np.r_[g * HEAD_DIM:(g + 1) * HEAD_DIM, (g + 4) * HEAD_DIM:(g + 5) * HEAD_DIM]
                           for g in range(ATT_HEADS // 2)])
    w = jnp.concatenate([a_val, a_glu, a_gate, r_q, r_k, r_v, r_gate, m_q[:, perm], m_k[:, perm], m_gate],
                        axis=-1).astype(BF16)
    return w, m_v.T.astype(BF16)


def _bf16_round(x):
    bits = np.asarray(x, np.float32).view(np.uint32).astype(np.uint64)
    bits = (bits + 0x7FFF + ((bits >> 16) & 1)) & 0xFFFF0000
    return bits.astype(np.uint32).view(np.float32)


def _q_bias_table():
    slopes = np.exp2(-8.0 * (np.arange(ATT_HEADS, dtype=np.float64) + 1.0) / ATT_HEADS)
    tab = np.zeros((ATT_HEADS, LANES), np.float32)
    for h in range(ATT_HEADS):
        base = HEAD_DIM if h < ATT_HEADS // 2 else 0
        rest = slopes[h] * LOG2E
        for t in range(BIAS_TERMS):
            term = float(_bf16_round(rest))
            tab[h, base + 2 * t] = term
            tab[h, base + 2 * t + 1] = term
            rest -= term
    return jnp.asarray(tab)


def kernel(x, norm_g, w_in, conv_w, conv_b, conv_ln_g, conv_ln_b, conv_pw_w, conv_pw_b, w_out, final_g):
    b, s, d = x.shape
    tm = min(512, s)
    qtab = _q_bias_table()
    tables = _ret_tables()
    fg = final_g.reshape(1, d)
    for layer in range(DEPTH):
        w, wvt = _pack_weights(w_in[layer])
        conv_params = (conv_w[layer], conv_b[layer].reshape(1, -1), conv_ln_g[layer].reshape(1, -1),
                       conv_ln_b[layer].reshape(1, -1), conv_pw_w[layer].astype(BF16),
                       conv_pw_b[layer].reshape(1, -1))
        ya, yr, sg, qa, ka, vt = _front(x, norm_g[layer].reshape(1, d), w, wvt, qtab, conv_params, tables, tm=tm)
        ym = _moba(qa, ka, vt)
        x = _outproj(ya, yr, ym, sg, x, w_out[layer].astype(BF16), fg, tm=min(2 * tm, s),
                     final=(layer == DEPTH - 1))
    return x
```

```python
import functools

import numpy as np
import jax
import jax.numpy as jnp
from jax import lax
from jax.experimental import pallas as pl
from jax.experimental.pallas import tpu as pltpu

D_MODEL = 1024
DEPTH = 2
HEAD_DIM = 64
CONV_WIDTH = 256
RET_WIDTH = 256
ATT_WIDTH = 512
RET_HEADS = RET_WIDTH // HEAD_DIM
ATT_HEADS = ATT_WIDTH // HEAD_DIM
CONV_K = 31
RET_CHUNK = 128
MOBA_BLOCK = 256
MOBA_TOPK = 3
RMS_EPS = 1e-6
LN_EPS = 1e-5
NEG = -1e30
LOG2E = 1.4426950408889634
BIAS_TERMS = 3

LANES = 128
SUBLANES = 8
VT_ROWS = HEAD_DIM + 16
CONV_ROWS = 64
HALO = 32
VMEM_LIMIT = 56 * 1024 * 1024

F32 = jnp.float32
BF16 = jnp.bfloat16

_C_AVAL, _C_AGLU, _C_AGATE = 0, 256, 512
_C_RQ, _C_RK, _C_RV, _C_RGATE = 768, 1024, 1280, 1536
_C_MQ, _C_MK, _C_MGATE, _C_END = 1792, 2304, 2816, 3328


def _sigmoid(x):
    return 1.0 / (1.0 + jnp.exp(-x))


def _dot(a, b):
    return jnp.dot(a, b, preferred_element_type=F32)


def _dot_nt(a, b):
    return lax.dot_general(a, b, (((1,), (1,)), ((), ())), preferred_element_type=F32)


def _dot_tn(a, b):
    return lax.dot_general(a, b, (((0,), (0,)), ((), ())), preferred_element_type=F32)


def _ret_tables():
    h = np.arange(RET_HEADS, dtype=np.float64)
    log_g = np.log(1.0 - np.exp2(-5.0 - h))
    i = np.arange(RET_CHUNK, dtype=np.float64)
    diff = i[:, None] - i[None, :]
    dec = np.where(diff >= 0, np.exp(log_g[:, None, None] * np.maximum(diff, 0.0)), 0.0)
    scale = HEAD_DIM ** -0.5
    dec_stack = (dec * scale).reshape(RET_HEADS * RET_CHUNK, RET_CHUNK)
    zeta = np.exp(log_g[:, None] * (RET_CHUNK - 1 - i))
    xi = np.exp(log_g[:, None] * (i + 1.0))
    zeta_tab = np.repeat(zeta.T, HEAD_DIM, axis=1) * scale
    xi_tab = np.repeat(xi.T, HEAD_DIM, axis=1)
    g_chunk = np.exp(log_g * RET_CHUNK)
    head_of = np.arange(RET_WIDTH) // HEAD_DIM
    same = head_of[:, None] == head_of[None, :]
    gtab = np.where(same, g_chunk[head_of][:, None], 0.0)
    f32 = lambda a: jnp.asarray(a, F32)
    return (f32(dec_stack), f32(zeta_tab), f32(xi_tab), f32(gtab), f32(same),
            jnp.asarray(same / HEAD_DIM, BF16))


def _group_mean(z, avg):
    hi = z.astype(BF16)
    lo = (z - hi.astype(F32)).astype(BF16)
    return _dot(hi, avg) + _dot(lo, avg)


def _front_kernel(x_ref, g_ref, w_ref, wvt_ref, qtab_ref,
                  cw_ref, cb_ref, lg_ref, lb_ref, pw_ref, pb_ref,
                  dec_ref, zeta_ref, xi_ref, gtab_ref, bmask_ref, avg_ref,
                  ya_ref, yr_ref, sg_ref, qa_ref, ka_ref, vt_ref,
                  hn_ref, ext_ref, sh_ref, sw_ref, r_ref, state_ref, *, tm):
    first = pl.program_id(1) == 0
    x = x_ref[0]
    ms = jnp.mean(x * x, axis=-1, keepdims=True)
    hn_ref[...] = (x * lax.rsqrt(ms + RMS_EPS) * g_ref[...]).astype(BF16)

    def proj(lo, hi):
        return _dot(hn_ref[...], w_ref[:, lo:hi])

    @pl.when(first)
    def _():
        ext_ref[0:HALO, :] = jnp.zeros((HALO, CONV_WIDTH), F32)
        state_ref[...] = jnp.zeros_like(state_ref)

    @pl.when(jnp.logical_not(first))
    def _():
        ext_ref[0:HALO, :] = ext_ref[tm:tm + HALO, :]

    a = proj(_C_AVAL, _C_AGATE)
    ext_ref[HALO:, :] = (a[:, :CONV_WIDTH] * _sigmoid(a[:, CONV_WIDTH:])).astype(BF16).astype(F32)
    span = HALO + tm - SUBLANES
    for r in range(1, SUBLANES):
        sh_ref[r - 1] = ext_ref[r:r + span, :]
    r_ref[...] = proj(_C_RQ, _C_RGATE).astype(BF16)

    def conv_chunk(c):
        base = HALO - (CONV_K - 1)
        r0 = c * CONV_ROWS
        acc = jnp.zeros((CONV_ROWS, CONV_WIDTH), F32) + cb_ref[...]
        for k in range(CONV_K):
            r = (base + k) % SUBLANES
            a0 = base + k - r + r0
            win = ext_ref[a0:a0 + CONV_ROWS, :] if r == 0 else sh_ref[r - 1, a0:a0 + CONV_ROWS, :]
            acc = acc + win * cw_ref[k:k + 1, :]
        mu = jnp.mean(acc, axis=-1, keepdims=True)
        dlt = acc - mu
        var = jnp.mean(dlt * dlt, axis=-1, keepdims=True)
        yn = dlt * lax.rsqrt(var + LN_EPS) * lg_ref[...] + lb_ref[...]
        sw_ref[r0:r0 + CONV_ROWS, :] = (yn * _sigmoid(yn)).astype(BF16)

    def conv_out():
        ya_ref[0] = (_dot(sw_ref[...], pw_ref[...]) + pb_ref[...]).astype(BF16)

    c = RET_CHUNK
    lane_head = lax.broadcasted_iota(jnp.int32, (c, RET_WIDTH), 1) // HEAD_DIM

    def retention():
        chunks = range(tm // c)
        sls = [slice(ci * c, (ci + 1) * c) for ci in chunks]
        avg = avg_ref[...]
        probs, kzs = [], []
        for sl in sls:
            qf = r_ref[sl, 0:RET_WIDTH].astype(F32)
            k = r_ref[sl, RET_WIDTH:2 * RET_WIDTH]
            qs = jnp.concatenate([jnp.where(lane_head == h, qf, 0.0).astype(BF16) for h in range(RET_HEADS)],
                                 axis=0)
            probs.append((_dot_nt(qs, k) * dec_ref[...]).astype(BF16))
            kzs.append((k.astype(F32) * zeta_ref[...]).astype(BF16))
        yield
        intras, states = [], []
        st = state_ref[...]
        for sl, p, kz in zip(sls, probs, kzs):
            v = r_ref[sl, 2 * RET_WIDTH:3 * RET_WIDTH]
            full = _dot(p, v)
            intra = jnp.zeros((c, RET_WIDTH), F32)
            for h in range(RET_HEADS):
                intra = jnp.where(lane_head == h, full[h * c:(h + 1) * c, :], intra)
            intras.append(intra)
            states.append(st.astype(BF16))
            st = _dot_tn(kz, v) * bmask_ref[...] + st * gtab_ref[...]
        state_ref[...] = st
        yield
        outs, mus = [], []
        for sl, intra, stb in zip(sls, intras, states):
            o = intra + _dot(r_ref[sl, 0:RET_WIDTH], stb) * xi_ref[...]
            outs.append(o)
            mus.append(_group_mean(o, avg))
        yield
        for sl, o, mu in zip(sls, outs, mus):
            dlt = o - mu
            var = _group_mean(dlt * dlt, avg)
            yr_ref[0, sl, :] = (dlt * lax.rsqrt(var + LN_EPS)).astype(BF16)
        yield

    def silu(t):
        return t * _sigmoid(t)

    def gates():
        sg_ref[0, :, 0:256] = silu(proj(_C_AGATE, _C_RQ)).astype(BF16)
        sg_ref[0, :, 256:512] = silu(proj(_C_RGATE, _C_MQ)).astype(BF16)

    def gate_m(half):
        lo = half * 256
        sg_ref[0, :, 512 + lo:768 + lo] = silu(proj(_C_MGATE + lo, _C_MGATE + lo + 256)).astype(BF16)

    def moba_qk(half):
        lane = lax.broadcasted_iota(jnp.int32, (tm, LANES), 1)
        low = lane < HEAD_DIM
        row = lax.broadcasted_iota(jnp.int32, (tm, LANES), 0) + pl.program_id(1) * tm
        pos_lo = (row % MOBA_BLOCK).astype(F32)
        pos_hi = (row - row % MOBA_BLOCK).astype(F32)
        zero = jnp.zeros((tm, LANES), F32)
        kb_lowhead = jnp.where((lane >= HEAD_DIM) & (lane < HEAD_DIM + 2 * BIAS_TERMS),
                               jnp.where(lane % 2 == 0, pos_lo, pos_hi), zero)
        kb_highhead = jnp.where(lane < 2 * BIAS_TERMS, jnp.where(lane % 2 == 0, pos_lo, pos_hi), zero)
        lo = half * 256
        mq = proj(_C_MQ + lo, _C_MQ + lo + 256) * (HEAD_DIM ** -0.5 * LOG2E)
        mk = proj(_C_MK + lo, _C_MK + lo + 256)
        for gl in range(2):
            g = 2 * half + gl
            qg = mq[:, gl * LANES:(gl + 1) * LANES]
            kg = mk[:, gl * LANES:(gl + 1) * LANES]
            qa_ref[0, g] = jnp.where(low, qg, qtab_ref[g:g + 1, :]).astype(BF16)
            qa_ref[0, g + 4] = jnp.where(low, qtab_ref[g + 4:g + 5, :], qg).astype(BF16)
            ka_ref[0, g] = jnp.where(low, kg, kb_lowhead).astype(BF16)
            ka_ref[0, g + 4] = jnp.where(low, kb_highhead, kg).astype(BF16)

    def moba_v(half):
        vt = _dot_nt(wvt_ref[half * 256:(half + 1) * 256, :], hn_ref[...])
        sub = lax.broadcasted_iota(jnp.int32, (VT_ROWS - HEAD_DIM, MOBA_BLOCK), 0)
        ones_row = jnp.where(sub == 0, 1.0, 0.0).astype(BF16)
        for hl in range(ATT_HEADS // 2):
            h = half * (ATT_HEADS // 2) + hl
            for cb in range(tm // MOBA_BLOCK):
                vt_ref[0, h, cb, 0:HEAD_DIM, :] = vt[hl * HEAD_DIM:(hl + 1) * HEAD_DIM,
                                                     cb * MOBA_BLOCK:(cb + 1) * MOBA_BLOCK].astype(BF16)
                vt_ref[0, h, cb, HEAD_DIM:, :] = ones_row

    n_conv = tm // CONV_ROWS
    mxu_work = [gates] + [functools.partial(f, half) for f in (gate_m, moba_qk, moba_v) for half in range(2)]
    ret_stages = retention()
    n_ret_stages = 4
    n_slots = len(mxu_work)
    ret_done = 0
    for slot in range(n_slots):
        if ret_done < n_ret_stages and slot * n_ret_stages >= ret_done * n_slots:
            next(ret_stages)
            ret_done += 1
        for cc in range(slot * n_conv // n_slots, (slot + 1) * n_conv // n_slots):
            conv_chunk(cc)
        mxu_work[slot]()
    for _ in ret_stages:
        pass
    conv_out()


def _front(x, g, w, wvt, qtab, conv_params, tables, *, tm):
    b, s, d = x.shape
    row_spec = lambda c: pl.BlockSpec((1, tm, c), lambda i, j: (i, j, 0))
    full2 = lambda a: pl.BlockSpec(a.shape, lambda i, j: (0, 0))
    consts = (g, w, wvt, qtab) + tuple(conv_params) + tuple(tables)
    out_shape = (
        jax.ShapeDtypeStruct((b, s, CONV_WIDTH), BF16),
        jax.ShapeDtypeStruct((b, s, RET_WIDTH), BF16),
        jax.ShapeDtypeStruct((b, s, D_MODEL), BF16),
        jax.ShapeDtypeStruct((b, ATT_HEADS, s, LANES), BF16),
        jax.ShapeDtypeStruct((b, ATT_HEADS, s, LANES), BF16),
        jax.ShapeDtypeStruct((b, ATT_HEADS, s // MOBA_BLOCK, VT_ROWS, MOBA_BLOCK), BF16),
    )
    out_specs = (
        row_spec(CONV_WIDTH), row_spec(RET_WIDTH), row_spec(D_MODEL),
        pl.BlockSpec((1, ATT_HEADS, tm, LANES), lambda i, j: (i, 0, j, 0)),
        pl.BlockSpec((1, ATT_HEADS, tm, LANES), lambda i, j: (i, 0, j, 0)),
        pl.BlockSpec((1, ATT_HEADS, tm // MOBA_BLOCK, VT_ROWS, MOBA_BLOCK), lambda i, j: (i, 0, j, 0, 0)),
    )
    return pl.pallas_call(
        functools.partial(_front_kernel, tm=tm),
        grid=(b, s // tm),
        in_specs=[row_spec(d)] + [full2(t) for t in consts],
        out_specs=out_specs,
        out_shape=out_shape,
        scratch_shapes=[
            pltpu.VMEM((tm, d), BF16),
            pltpu.VMEM((HALO + tm, CONV_WIDTH), F32),
            pltpu.VMEM((SUBLANES - 1, HALO + tm - SUBLANES, CONV_WIDTH), F32),
            pltpu.VMEM((tm, CONV_WIDTH), BF16),
            pltpu.VMEM((tm, 3 * RET_WIDTH), BF16),
            pltpu.VMEM((RET_WIDTH, RET_WIDTH), F32),
        ],
        compiler_params=pltpu.CompilerParams(
            dimension_semantics=("parallel", "arbitrary"), vmem_limit_bytes=VMEM_LIMIT),
        name="front",
    )(x, *consts)


def _moba_kernel(q_ref, k_ref, vt_ref, y_ref, km_ref, s0_ref, s1_ref, m_ref, mb_ref, *, nblk, qt):
    s_refs = (s0_ref, s1_ref)
    n_tiles = nblk // qt
    g = pl.program_id(1)
    last = ATT_HEADS * n_tiles
    head = jnp.minimum(g // n_tiles, ATT_HEADS - 1)
    blk = MOBA_BLOCK
    nq = qt * blk
    lane = lax.broadcasted_iota(jnp.int32, (1, LANES), 1)
    real_lanes = (lane < HEAD_DIM) == (head < ATT_HEADS // 2)

    @pl.when((g % n_tiles == 0) & (g < last))
    def _():
        rows = []
        for i in range(nblk):
            kb = k_ref[0, 0, i * blk:(i + 1) * blk, :].astype(F32)
            rows.append(jnp.sum(kb, axis=0, keepdims=True) * (1.0 / blk))
        km = jnp.where(real_lanes, jnp.concatenate(rows, axis=0), 0.0)
        hi = km.astype(BF16)
        km_ref[0:nblk, :] = hi
        km_ref[nblk:2 * nblk, :] = (km - hi.astype(F32)).astype(BF16)

    blk_id = lax.broadcasted_iota(jnp.int32, (nblk, nq), 0)
    q_blk = lax.broadcasted_iota(jnp.int32, (nblk, nq), 1) // blk
    kdiff = (lax.broadcasted_iota(jnp.int32, (blk, blk), 0)
             - lax.broadcasted_iota(jnp.int32, (blk, blk), 1))

    def logits_stage(tt, slot):
        j0 = tt * qt
        q_all = q_ref[0, 0]
        g2 = _dot_nt(km_ref[...], q_all)
        gate = g2[0:nblk] + g2[nblk:2 * nblk]
        past = blk_id < q_blk + j0
        gm = jnp.where(past, gate, NEG)
        cnt = jnp.zeros((nblk, nq), jnp.int32)
        for i2 in range(nblk):
            r = gm[i2:i2 + 1, :]
            beats = (r > gm) | ((r == gm) & (blk_id > i2))
            cnt = cnt + beats.astype(jnp.int32)
        sel = ((cnt < MOBA_TOPK) & past) | (blk_id == q_blk + j0)
        mb_all = jnp.where(sel, 0.0, NEG)
        mb_ref[slot] = mb_all

        off = 0
        for jl in range(qt):
            j = j0 + jl
            q = q_all[jl * blk:(jl + 1) * blk, :]
            mb = mb_all[:, jl * blk:(jl + 1) * blk]
            m8 = jnp.full((8, blk), NEG, F32)
            for i in range(j + 1):
                s = _dot_nt(k_ref[0, 0, i * blk:(i + 1) * blk, :], q)
                if i == j:
                    s = jnp.where(kdiff > 0, NEG, s)
                s_refs[slot][off + i] = s
                m8 = jnp.maximum(m8, jnp.max(s.reshape(blk // 8, 8, blk), axis=0) + mb[i:i + 1, :])
                if i == j:
                    m_ref[slot, jl] = jnp.max(m8, axis=0, keepdims=True)
                yield
            off += j + 1

    def value_stage(tt, slot):
        j0 = tt * qt
        off = 0
        for jl in range(qt):
            j = j0 + jl
            m = m_ref[slot, jl]
            acc = jnp.zeros((VT_ROWS, blk), F32)
            for i in range(j + 1):
                shift = m - mb_ref[slot, i:i + 1, jl * blk:(jl + 1) * blk]
                p = jnp.exp2(s_refs[slot][off + i] - shift).astype(BF16)
                acc = acc + _dot(vt_ref[0, 0, i], p)
                if i == j:
                    o = acc[0:HEAD_DIM, :] * (1.0 / acc[HEAD_DIM:HEAD_DIM + 1, :])
                    y_ref[0, 0, :, jl * blk:(jl + 1) * blk] = o.astype(BF16)
                yield
            off += j + 1

    n_tiles = nblk // qt

    def n_pairs(tt):
        return sum(tt * qt + jl + 1 for jl in range(qt))

    def step(logits_tile, value_tile):
        streams = []
        if logits_tile is not None:
            slot = (n_tiles - 1 - logits_tile) % 2
            streams.append((logits_stage(logits_tile, slot), n_pairs(logits_tile)))
        if value_tile is not None:
            slot = (n_tiles - 1 - value_tile) % 2
            streams.append((value_stage(value_tile, slot), n_pairs(value_tile)))
        done = [0] * len(streams)
        while any(d < n for d, (_, n) in zip(done, streams)):
            k = min((d / n, idx) for idx, (d, (_, n)) in enumerate(zip(done, streams)) if d < n)[1]
            next(streams[k][0])
            done[k] += 1
        for gen, _ in streams:
            for _ in gen:
                pass

    pl.when(g == 0)(functools.partial(step, n_tiles - 1, None))
    pl.when(g == last)(functools.partial(step, None, 0))
    for r in range(n_tiles):
        pl.when((g > 0) & (g < last) & (g % n_tiles == r))(
            functools.partial(step, n_tiles - 1 - r, (n_tiles - r) % n_tiles))


def _moba(qa, ka, vt):
    b, nh, s, _ = qa.shape
    nblk = s // MOBA_BLOCK
    qt = min(8, nblk // 2)
    n_tiles = nblk // qt
    pairs = [sum(tt * qt + jl + 1 for jl in range(qt)) for tt in range(n_tiles)]
    slot_blocks = [max(p for tt, p in enumerate(pairs) if (n_tiles - 1 - tt) % 2 == sl) for sl in range(2)]
    assert nh == ATT_HEADS and n_tiles % 2 == 0, (nh, n_tiles)
    last = nh * n_tiles

    def logits_head(g):
        return jnp.minimum(g // n_tiles, nh - 1)

    def logits_tile(g):
        return jnp.where(g < last, n_tiles - 1 - g % n_tiles, 0)

    def value_step(g):
        return jnp.maximum(g - 1, 0)

    return pl.pallas_call(
        functools.partial(_moba_kernel, nblk=nblk, qt=qt),
        grid=(b, last + 1),
        in_specs=[
            pl.BlockSpec((1, 1, qt * MOBA_BLOCK, LANES), lambda i, g: (i, logits_head(g), logits_tile(g), 0)),
            pl.BlockSpec((1, 1, s, LANES), lambda i, g: (i, logits_head(g), 0, 0)),
            pl.BlockSpec((1, 1, nblk, VT_ROWS, MOBA_BLOCK),
                         lambda i, g: (i, value_step(g) // n_tiles, 0, 0, 0)),
        ],
        out_specs=pl.BlockSpec((1, 1, HEAD_DIM, qt * MOBA_BLOCK),
                               lambda i, g: (i, value_step(g) // n_tiles, 0,
                                             n_tiles - 1 - value_step(g) % n_tiles)),
        out_shape=jax.ShapeDtypeStruct((b, nh, HEAD_DIM, s), BF16),
        scratch_shapes=[
            pltpu.VMEM((2 * nblk, LANES), BF16),
            pltpu.VMEM((slot_blocks[0], MOBA_BLOCK, MOBA_BLOCK), F32),
            pltpu.VMEM((slot_blocks[1], MOBA_BLOCK, MOBA_BLOCK), F32),
            pltpu.VMEM((2, qt, 1, MOBA_BLOCK), F32),
            pltpu.VMEM((2, nblk, qt * MOBA_BLOCK), F32),
        ],
        compiler_params=pltpu.CompilerParams(
            dimension_semantics=("parallel", "arbitrary"), vmem_limit_bytes=VMEM_LIMIT),
        name="moba",
    )(qa, ka, vt)


def _outproj_kernel(ya_ref, yr_ref, ymt_ref, sg_ref, x_ref, w_ref, fg_ref, o_ref, *, final):
    tm = x_ref.shape[1]
    ym = ymt_ref[0].reshape(ATT_WIDTH, tm).astype(F32).T
    mix = jnp.concatenate([ya_ref[0].astype(F32), yr_ref[0].astype(F32), ym], axis=-1)
    mix = (mix * sg_ref[0].astype(F32)).astype(BF16)
    xn = x_ref[0] + _dot(mix, w_ref[...])
    if final:
        ms = jnp.mean(xn * xn, axis=-1, keepdims=True)
        xn = xn * lax.rsqrt(ms + RMS_EPS) * fg_ref[...]
    o_ref[0] = xn


def _outproj(ya, yr, ym, sg, x, w, fg, *, tm, final):
    b, s, d = x.shape
    row_spec = lambda c: pl.BlockSpec((1, tm, c), lambda i, j: (i, j, 0))
    full2 = lambda a: pl.BlockSpec(a.shape, lambda i, j: (0, 0))
    return pl.pallas_call(
        functools.partial(_outproj_kernel, final=final),
        grid=(b, s // tm),
        in_specs=[row_spec(CONV_WIDTH), row_spec(RET_WIDTH),
                  pl.BlockSpec((1, ATT_HEADS, HEAD_DIM, tm), lambda i, j: (i, 0, 0, j)),
                  row_spec(d), row_spec(d), full2(w), full2(fg)],
        out_specs=row_spec(d),
        out_shape=jax.ShapeDtypeStruct((b, s, d), F32),
        compiler_params=pltpu.CompilerParams(
            dimension_semantics=("parallel", "parallel"), vmem_limit_bytes=VMEM_LIMIT),
        name="outproj",
    )(ya, yr, ym, sg, x, w, fg)


def _pack_weights(w_in_l):
    sizes = (CONV_WIDTH,) * 3 + (RET_WIDTH,) * 4 + (ATT_WIDTH,) * 4
    pts = np.cumsum(sizes)[:-1]
    a_val, a_glu, a_gate, r_q, r_k, r_v, r_gate, m_q, m_k, m_v, m_gate = jnp.split(w_in_l, pts, axis=-1)
    perm = np.concatenate([np.r_[g * HEAD_DIM:(g + 1) * HEAD_DIM, (g + 4) * HEAD_DIM:(g + 5) * HEAD_DIM]
                           for g in range(ATT_HEADS // 2)])
    w = jnp.concatenate([a_val, a_glu, a_gate, r_q, r_k, r_v, r_gate, m_q[:, perm], m_k[:, perm], m_gate],
                        axis=-1).astype(BF16)
    return w, m_v.T.astype(BF16)


def _bf16_round(x):
    bits = np.asarray(x, np.float32).view(np.uint32).astype(np.uint64)
    bits = (bits + 0x7FFF + ((bits >> 16) & 1)) & 0xFFFF0000
    return bits.astype(np.uint32).view(np.float32)


def _q_bias_table():
    slopes = np.exp2(-8.0 * (np.arange(ATT_HEADS, dtype=np.float64) + 1.0) / ATT_HEADS)
    tab = np.zeros((ATT_HEADS, LANES), np.float32)
    for h in range(ATT_HEADS):
        base = HEAD_DIM if h < ATT_HEADS // 2 else 0
        rest = slopes[h] * LOG2E
        for t in range(BIAS_TERMS):
            term = float(_bf16_round(rest))
            tab[h, base + 2 * t] = term
            tab[h, base + 2 * t + 1] = term
            rest -= term
    return jnp.asarray(tab)


def kernel(x, norm_g, w_in, conv_w, conv_b, conv_ln_g, conv_ln_b, conv_pw_w, conv_pw_b, w_out, final_g):
    b, s, d = x.shape
    tm = min(512, s)
    qtab = _q_bias_table()
    tables = _ret_tables()
    fg = final_g.reshape(1, d)
    for layer in range(DEPTH):
        w, wvt = _pack_weights(w_in[layer])
        conv_params = (conv_w[layer], conv_b[layer].reshape(1, -1), conv_ln_g[layer].reshape(1, -1),
                       conv_ln_b[layer].reshape(1, -1), conv_pw_w[layer].astype(BF16),
                       conv_pw_b[layer].reshape(1, -1))
        ya, yr, sg, qa, ka, vt = _front(x, norm_g[layer].reshape(1, d), w, wvt, qtab, conv_params, tables, tm=tm)
        ym = _moba(qa, ka, vt)
        x = _outproj(ya, yr, ym, sg, x, w_out[layer].astype(BF16), fg, tm=min(2 * tm, s),
                     final=(layer == DEPTH - 1))
    return x
```

```python
import functools

import numpy as np
import jax
import jax.numpy as jnp
from jax import lax
from jax.experimental import pallas as pl
from jax.experimental.pallas import tpu as pltpu

D_MODEL = 1024
DEPTH = 2
HEAD_DIM = 64
CONV_WIDTH = 256
RET_WIDTH = 256
ATT_WIDTH = 512
RET_HEADS = RET_WIDTH // HEAD_DIM
ATT_HEADS = ATT_WIDTH // HEAD_DIM
CONV_K = 31
RET_CHUNK = 128
MOBA_BLOCK = 256
MOBA_TOPK = 3
RMS_EPS = 1e-6
LN_EPS = 1e-5
NEG = -1e30
LOG2E = 1.4426950408889634
BIAS_TERMS = 3

LANES = 128
SUBLANES = 8
VT_ROWS = HEAD_DIM + 16
CONV_ROWS = 64
HALO = 32
VMEM_LIMIT = 56 * 1024 * 1024

F32 = jnp.float32
BF16 = jnp.bfloat16

_C_AVAL, _C_AGLU, _C_AGATE = 0, 256, 512
_C_RQ, _C_RK, _C_RV, _C_RGATE = 768, 1024, 1280, 1536
_C_MQ, _C_MK, _C_MV, _C_MGATE, _C_END = 1792, 2304, 2816, 3328, 3840


def _sigmoid(x):
    return 1.0 / (1.0 + jnp.exp(-x))


def _dot(a, b):
    return jnp.dot(a, b, preferred_element_type=F32)


def _dot_nt(a, b):
    return lax.dot_general(a, b, (((1,), (1,)), ((), ())), preferred_element_type=F32)


def _dot_tn(a, b):
    return lax.dot_general(a, b, (((0,), (0,)), ((), ())), preferred_element_type=F32)


def _ret_tables():
    h = np.arange(RET_HEADS, dtype=np.float64)
    log_g = np.log(1.0 - np.exp2(-5.0 - h))
    i = np.arange(RET_CHUNK, dtype=np.float64)
    diff = i[:, None] - i[None, :]
    dec = np.where(diff >= 0, np.exp(log_g[:, None, None] * np.maximum(diff, 0.0)), 0.0)
    scale = HEAD_DIM ** -0.5
    dec_stack = (dec * scale).reshape(RET_HEADS * RET_CHUNK, RET_CHUNK)
    zeta = np.exp(log_g[:, None] * (RET_CHUNK - 1 - i))
    xi = np.exp(log_g[:, None] * (i + 1.0))
    zeta_tab = np.repeat(zeta.T, HEAD_DIM, axis=1) * scale
    xi_tab = np.repeat(xi.T, HEAD_DIM, axis=1)
    g_chunk = np.exp(log_g * RET_CHUNK)
    head_of = np.arange(RET_WIDTH) // HEAD_DIM
    same = head_of[:, None] == head_of[None, :]
    gtab = np.where(same, g_chunk[head_of][:, None], 0.0)
    f32 = lambda a: jnp.asarray(a, F32)
    return (f32(dec_stack), f32(zeta_tab), f32(xi_tab), f32(gtab), f32(same),
            jnp.asarray(same / HEAD_DIM, BF16))


def _group_mean(z, avg):
    hi = z.astype(BF16)
    lo = (z - hi.astype(F32)).astype(BF16)
    return _dot(hi, avg) + _dot(lo, avg)


def _front_kernel(x_ref, g_ref, w_ref, wvt_ref, qtab_ref,
                  cw_ref, cb_ref, lg_ref, lb_ref, pw_ref, pb_ref,
                  dec_ref, zeta_ref, xi_ref, gtab_ref, bmask_ref, avg_ref,
                  ya_ref, yr_ref, sg_ref, qa_ref, ka_ref, vt_ref,
                  hn_ref, ext_ref, sh_ref, sw_ref, r_ref, state_ref, *, tm):
    first = pl.program_id(1) == 0
    x = x_ref[0]
    ms = jnp.mean(x * x, axis=-1, keepdims=True)
    hn_ref[...] = (x * lax.rsqrt(ms + RMS_EPS) * g_ref[...]).astype(BF16)

    def proj(lo, hi):
        return _dot(hn_ref[...], w_ref[:, lo:hi])

    @pl.when(first)
    def _():
        ext_ref[0:HALO, :] = jnp.zeros((HALO, CONV_WIDTH), F32)
        state_ref[...] = jnp.zeros_like(state_ref)

    @pl.when(jnp.logical_not(first))
    def _():
        ext_ref[0:HALO, :] = ext_ref[tm:tm + HALO, :]

    a = proj(_C_AVAL, _C_AGATE)
    ext_ref[HALO:, :] = (a[:, :CONV_WIDTH] * _sigmoid(a[:, CONV_WIDTH:])).astype(BF16).astype(F32)
    span = HALO + tm - SUBLANES
    for r in range(1, SUBLANES):
        sh_ref[r - 1] = ext_ref[r:r + span, :]
    r_ref[...] = proj(_C_RQ, _C_RGATE).astype(BF16)

    def conv_chunk(c):
        base = HALO - (CONV_K - 1)
        r0 = c * CONV_ROWS
        acc = jnp.zeros((CONV_ROWS, CONV_WIDTH), F32) + cb_ref[...]
        for k in range(CONV_K):
            r = (base + k) % SUBLANES
            a0 = base + k - r + r0
            win = ext_ref[a0:a0 + CONV_ROWS, :] if r == 0 else sh_ref[r - 1, a0:a0 + CONV_ROWS, :]
            acc = acc + win * cw_ref[k:k + 1, :]
        mu = jnp.mean(acc, axis=-1, keepdims=True)
        dlt = acc - mu
        var = jnp.mean(dlt * dlt, axis=-1, keepdims=True)
        yn = dlt * lax.rsqrt(var + LN_EPS) * lg_ref[...] + lb_ref[...]
        sw_ref[r0:r0 + CONV_ROWS, :] = (yn * _sigmoid(yn)).astype(BF16)

    def conv_out():
        ya_ref[0] = (_dot(sw_ref[...], pw_ref[...]) + pb_ref[...]).astype(BF16)

    c = RET_CHUNK
    lane_head = lax.broadcasted_iota(jnp.int32, (c, RET_WIDTH), 1) // HEAD_DIM

    def retention():
        chunks = range(tm // c)
        sls = [slice(ci * c, (ci + 1) * c) for ci in chunks]
        avg = avg_ref[...]
        probs, kzs = [], []
        for sl in sls:
            qf = r_ref[sl, 0:RET_WIDTH].astype(F32)
            k = r_ref[sl, RET_WIDTH:2 * RET_WIDTH]
            qs = jnp.concatenate([jnp.where(lane_head == h, qf, 0.0).astype(BF16) for h in range(RET_HEADS)],
                                 axis=0)
            probs.append((_dot_nt(qs, k) * dec_ref[...]).astype(BF16))
            kzs.append((k.astype(F32) * zeta_ref[...]).astype(BF16))
        yield
        intras, states = [], []
        st = state_ref[...]
        for sl, p, kz in zip(sls, probs, kzs):
            v = r_ref[sl, 2 * RET_WIDTH:3 * RET_WIDTH]
            full = _dot(p, v)
            intra = jnp.zeros((c, RET_WIDTH), F32)
            for h in range(RET_HEADS):
                intra = jnp.where(lane_head == h, full[h * c:(h + 1) * c, :], intra)
            intras.append(intra)
            states.append(st.astype(BF16))
            st = _dot_tn(kz, v) * bmask_ref[...] + st * gtab_ref[...]
        state_ref[...] = st
        yield
        outs, mus = [], []
        for sl, intra, stb in zip(sls, intras, states):
            o = intra + _dot(r_ref[sl, 0:RET_WIDTH], stb) * xi_ref[...]
            outs.append(o)
            mus.append(_group_mean(o, avg))
        yield
        for sl, o, mu in zip(sls, outs, mus):
            dlt = o - mu
            var = _group_mean(dlt * dlt, avg)
            yr_ref[0, sl, :] = (dlt * lax.rsqrt(var + LN_EPS)).astype(BF16)
        yield

    def silu(t):
        return t * _sigmoid(t)

    def gates():
        sg_ref[0, :, 0:256] = silu(proj(_C_AGATE, _C_RQ)).astype(BF16)
        sg_ref[0, :, 256:512] = silu(proj(_C_RGATE, _C_MQ)).astype(BF16)

    def gate_m(half):
        lo = half * 256
        sg_ref[0, :, 512 + lo:768 + lo] = silu(proj(_C_MGATE + lo, _C_MGATE + lo + 256)).astype(BF16)

    def moba_qk(half):
        lane = lax.broadcasted_iota(jnp.int32, (tm, LANES), 1)
        low = lane < HEAD_DIM
        row = lax.broadcasted_iota(jnp.int32, (tm, LANES), 0) + pl.program_id(1) * tm
        pos_lo = (row % MOBA_BLOCK).astype(F32)
        pos_hi = (row - row % MOBA_BLOCK).astype(F32)
        zero = jnp.zeros((tm, LANES), F32)
        kb_lowhead = jnp.where((lane >= HEAD_DIM) & (lane < HEAD_DIM + 2 * BIAS_TERMS),
                               jnp.where(lane % 2 == 0, pos_lo, pos_hi), zero)
        kb_highhead = jnp.where(lane < 2 * BIAS_TERMS, jnp.where(lane % 2 == 0, pos_lo, pos_hi), zero)
        lo = half * 256
        mq = proj(_C_MQ + lo, _C_MQ + lo + 256) * (HEAD_DIM ** -0.5 * LOG2E)
        mk = proj(_C_MK + lo, _C_MK + lo + 256)
        for gl in range(2):
            g = 2 * half + gl
            qg = mq[:, gl * LANES:(gl + 1) * LANES]
            kg = mk[:, gl * LANES:(gl + 1) * LANES]
            qa_ref[0, 2 * g] = jnp.where(low, qg, qtab_ref[2 * g:2 * g + 1, :]).astype(BF16)
            qa_ref[0, 2 * g + 1] = jnp.where(low, qtab_ref[2 * g + 1:2 * g + 2, :], qg).astype(BF16)
            ka_ref[0, 2 * g] = jnp.where(low, kg, kb_lowhead).astype(BF16)
            ka_ref[0, 2 * g + 1] = jnp.where(low, kb_highhead, kg).astype(BF16)

    def moba_v(half):
        vt = _dot_nt(wvt_ref[half * 256:(half + 1) * 256, :], hn_ref[...])
        sub = lax.broadcasted_iota(jnp.int32, (VT_ROWS - HEAD_DIM, MOBA_BLOCK), 0)
        ones_row = jnp.where(sub == 0, 1.0, 0.0).astype(BF16)
        for hl in range(ATT_HEADS // 2):
            h = half * (ATT_HEADS // 2) + hl
            for cb in range(tm // MOBA_BLOCK):
                vt_ref[0, h, cb, 0:HEAD_DIM, :] = vt[hl * HEAD_DIM:(hl + 1) * HEAD_DIM,
                                                     cb * MOBA_BLOCK:(cb + 1) * MOBA_BLOCK].astype(BF16)
                vt_ref[0, h, cb, HEAD_DIM:, :] = ones_row

    n_conv = tm // CONV_ROWS
    mxu_work = [gates] + [functools.partial(f, half) for f in (gate_m, moba_qk, moba_v) for half in range(2)]
    ret_stages = retention()
    n_ret_stages = 4
    n_slots = len(mxu_work)
    ret_done = 0
    for slot in range(n_slots):
        if ret_done < n_ret_stages and slot * n_ret_stages >= ret_done * n_slots:
            next(ret_stages)
            ret_done += 1
        for cc in range(slot * n_conv // n_slots, (slot + 1) * n_conv // n_slots):
            conv_chunk(cc)
        mxu_work[slot]()
    for _ in ret_stages:
        pass
    conv_out()


def _front(x, g, w, wvt, qtab, conv_params, tables, *, tm):
    b, s, d = x.shape
    row_spec = lambda c: pl.BlockSpec((1, tm, c), lambda i, j: (i, j, 0))
    full2 = lambda a: pl.BlockSpec(a.shape, lambda i, j: (0, 0))
    consts = (g, w, wvt, qtab) + tuple(conv_params) + tuple(tables)
    out_shape = (
        jax.ShapeDtypeStruct((b, s, CONV_WIDTH), BF16),
        jax.ShapeDtypeStruct((b, s, RET_WIDTH), BF16),
        jax.ShapeDtypeStruct((b, s, D_MODEL), BF16),
        jax.ShapeDtypeStruct((b, ATT_HEADS, s, LANES), BF16),
        jax.ShapeDtypeStruct((b, ATT_HEADS, s, LANES), BF16),
        jax.ShapeDtypeStruct((b, ATT_HEADS, s // MOBA_BLOCK, VT_ROWS, MOBA_BLOCK), BF16),
    )
    out_specs = (
        row_spec(CONV_WIDTH), row_spec(RET_WIDTH), row_spec(D_MODEL),
        pl.BlockSpec((1, ATT_HEADS, tm, LANES), lambda i, j: (i, 0, j, 0)),
        pl.BlockSpec((1, ATT_HEADS, tm, LANES), lambda i, j: (i, 0, j, 0)),
        pl.BlockSpec((1, ATT_HEADS, tm // MOBA_BLOCK, VT_ROWS, MOBA_BLOCK), lambda i, j: (i, 0, j, 0, 0)),
    )
    return pl.pallas_call(
        functools.partial(_front_kernel, tm=tm),
        grid=(b, s // tm),
        in_specs=[row_spec(d)] + [full2(t) for t in consts],
        out_specs=out_specs,
        out_shape=out_shape,
        scratch_shapes=[
            pltpu.VMEM((tm, d), BF16),
            pltpu.VMEM((HALO + tm, CONV_WIDTH), F32),
            pltpu.VMEM((SUBLANES - 1, HALO + tm - SUBLANES, CONV_WIDTH), F32),
            pltpu.VMEM((tm, CONV_WIDTH), BF16),
            pltpu.VMEM((tm, 3 * RET_WIDTH), BF16),
            pltpu.VMEM((RET_WIDTH, RET_WIDTH), F32),
        ],
        compiler_params=pltpu.CompilerParams(
            dimension_semantics=("parallel", "arbitrary"), vmem_limit_bytes=VMEM_LIMIT),
        name="front",
    )(x, *consts)


def _moba_kernel(q_ref, k_ref, vt_ref, y_ref, km_ref, s0_ref, s1_ref, m_ref, mb_ref, *, nblk, qt):
    s_refs = (s0_ref, s1_ref)
    n_tiles = nblk // qt
    g = pl.program_id(1)
    last = ATT_HEADS * n_tiles
    head = jnp.minimum(g // n_tiles, ATT_HEADS - 1)
    blk = MOBA_BLOCK
    nq = qt * blk
    lane = lax.broadcasted_iota(jnp.int32, (1, LANES), 1)
    real_lanes = (lane < HEAD_DIM) == (head % 2 == 0)

    @pl.when((g % n_tiles == 0) & (g < last))
    def _():
        rows = []
        for i in range(nblk):
            kb = k_ref[0, 0, i * blk:(i + 1) * blk, :].astype(F32)
            rows.append(jnp.sum(kb, axis=0, keepdims=True) * (1.0 / blk))
        km = jnp.where(real_lanes, jnp.concatenate(rows, axis=0), 0.0)
        hi = km.astype(BF16)
        km_ref[0:nblk, :] = hi
        km_ref[nblk:2 * nblk, :] = (km - hi.astype(F32)).astype(BF16)

    blk_id = lax.broadcasted_iota(jnp.int32, (nblk, nq), 0)
    q_blk = lax.broadcasted_iota(jnp.int32, (nblk, nq), 1) // blk
    kdiff = (lax.broadcasted_iota(jnp.int32, (blk, blk), 0)
             - lax.broadcasted_iota(jnp.int32, (blk, blk), 1))

    def logits_stage(tt, slot):
        j0 = tt * qt
        q_all = q_ref[0, 0]
        g2 = _dot_nt(km_ref[...], q_all)
        gate = g2[0:nblk] + g2[nblk:2 * nblk]
        past = blk_id < q_blk + j0
        gm = jnp.where(past, gate, NEG)
        cnt = jnp.zeros((nblk, nq), jnp.int32)
        for i2 in range(nblk):
            r = gm[i2:i2 + 1, :]
            beats = (r > gm) | ((r == gm) & (blk_id > i2))
            cnt = cnt + beats.astype(jnp.int32)
        sel = ((cnt < MOBA_TOPK) & past) | (blk_id == q_blk + j0)
        mb_all = jnp.where(sel, 0.0, NEG)
        mb_ref[slot] = mb_all

        off = 0
        for jl in range(qt):
            j = j0 + jl
            q = q_all[jl * blk:(jl + 1) * blk, :]
            mb = mb_all[:, jl * blk:(jl + 1) * blk]
            m8 = jnp.full((8, blk), NEG, F32)
            for i in range(j + 1):
                s = _dot_nt(k_ref[0, 0, i * blk:(i + 1) * blk, :], q)
                if i == j:
                    s = jnp.where(kdiff > 0, NEG, s)
                s_refs[slot][off + i] = s
                m8 = jnp.maximum(m8, jnp.max(s.reshape(blk // 8, 8, blk), axis=0) + mb[i:i + 1, :])
                if i == j:
                    m_ref[slot, jl] = jnp.max(m8, axis=0, keepdims=True)
                yield
            off += j + 1

    def value_stage(tt, slot):
        j0 = tt * qt
        off = 0
        for jl in range(qt):
            j = j0 + jl
            m = m_ref[slot, jl]
            acc = jnp.zeros((VT_ROWS, blk), F32)
            for i in range(j + 1):
                shift = m - mb_ref[slot, i:i + 1, jl * blk:(jl + 1) * blk]
                p = jnp.exp2(s_refs[slot][off + i] - shift).astype(BF16)
                acc = acc + _dot(vt_ref[0, 0, i], p)
                if i == j:
                    o = acc[0:HEAD_DIM, :] * (1.0 / acc[HEAD_DIM:HEAD_DIM + 1, :])
                    y_ref[0, 0, :, jl * blk:(jl + 1) * blk] = o.astype(BF16)
                yield
            off += j + 1

    n_tiles = nblk // qt

    def n_pairs(tt):
        return sum(tt * qt + jl + 1 for jl in range(qt))

    def step(logits_tile, value_tile):
        streams = []
        if logits_tile is not None:
            slot = (n_tiles - 1 - logits_tile) % 2
            streams.append((logits_stage(logits_tile, slot), n_pairs(logits_tile)))
        if value_tile is not None:
            slot = (n_tiles - 1 - value_tile) % 2
            streams.append((value_stage(value_tile, slot), n_pairs(value_tile)))
        done = [0] * len(streams)
        while any(d < n for d, (_, n) in zip(done, streams)):
            k = min((d / n, idx) for idx, (d, (_, n)) in enumerate(zip(done, streams)) if d < n)[1]
            next(streams[k][0])
            done[k] += 1
        for gen, _ in streams:
            for _ in gen:
                pass

    pl.when(g == 0)(functools.partial(step, n_tiles - 1, None))
    pl.when(g == last)(functools.partial(step, None, 0))
    for r in range(n_tiles):
        pl.when((g > 0) & (g < last) & (g % n_tiles == r))(
            functools.partial(step, n_tiles - 1 - r, (n_tiles - r) % n_tiles))


def _moba(qa, ka, vt):
    b, nh, s, _ = qa.shape
    nblk = s // MOBA_BLOCK
    qt = min(8, nblk // 2)
    n_tiles = nblk // qt
    pairs = [sum(tt * qt + jl + 1 for jl in range(qt)) for tt in range(n_tiles)]
    slot_blocks = [max(p for tt, p in enumerate(pairs) if (n_tiles - 1 - tt) % 2 == sl) for sl in range(2)]
    assert nh == ATT_HEADS and n_tiles % 2 == 0, (nh, n_tiles)
    last = nh * n_tiles

    def logits_head(g):
        return jnp.minimum(g // n_tiles, nh - 1)

    def logits_tile(g):
        return jnp.where(g < last, n_tiles - 1 - g % n_tiles, 0)

    def value_step(g):
        return jnp.maximum(g - 1, 0)

    return pl.pallas_call(
        functools.partial(_moba_kernel, nblk=nblk, qt=qt),
        grid=(b, last + 1),
        in_specs=[
            pl.BlockSpec((1, 1, qt * MOBA_BLOCK, LANES), lambda i, g: (i, logits_head(g), logits_tile(g), 0)),
            pl.BlockSpec((1, 1, s, LANES), lambda i, g: (i, logits_head(g), 0, 0)),
            pl.BlockSpec((1, 1, nblk, VT_ROWS, MOBA_BLOCK),
                         lambda i, g: (i, value_step(g) // n_tiles, 0, 0, 0)),
        ],
        out_specs=pl.BlockSpec((1, 1, HEAD_DIM, qt * MOBA_BLOCK),
                               lambda i, g: (i, value_step(g) // n_tiles, 0,
                                             n_tiles - 1 - value_step(g) % n_tiles)),
        out_shape=jax.ShapeDtypeStruct((b, nh, HEAD_DIM, s), BF16),
        scratch_shapes=[
            pltpu.VMEM((2 * nblk, LANES), BF16),
            pltpu.VMEM((slot_blocks[0], MOBA_BLOCK, MOBA_BLOCK), F32),
            pltpu.VMEM((slot_blocks[1], MOBA_BLOCK, MOBA_BLOCK), F32),
            pltpu.VMEM((2, qt, 1, MOBA_BLOCK), F32),
            pltpu.VMEM((2, nblk, qt * MOBA_BLOCK), F32),
        ],
        compiler_params=pltpu.CompilerParams(
            dimension_semantics=("parallel", "arbitrary"), vmem_limit_bytes=VMEM_LIMIT),
        name="moba",
    )(qa, ka, vt)


def _outproj_kernel(ya_ref, yr_ref, ymt_ref, sg_ref, x_ref, w_ref, fg_ref, o_ref, *, final):
    tm = x_ref.shape[1]
    ym = ymt_ref[0].reshape(ATT_WIDTH, tm).astype(F32).T
    mix = jnp.concatenate([ya_ref[0].astype(F32), yr_ref[0].astype(F32), ym], axis=-1)
    mix = (mix * sg_ref[0].astype(F32)).astype(BF16)
    xn = x_ref[0] + _dot(mix, w_ref[...])
    if final:
        ms = jnp.mean(xn * xn, axis=-1, keepdims=True)
        xn = xn * lax.rsqrt(ms + RMS_EPS) * fg_ref[...]
    o_ref[0] = xn


def _outproj(ya, yr, ym, sg, x, w, fg, *, tm, final):
    b, s, d = x.shape
    row_spec = lambda c: pl.BlockSpec((1, tm, c), lambda i, j: (i, j, 0))
    full2 = lambda a: pl.BlockSpec(a.shape, lambda i, j: (0, 0))
    return pl.pallas_call(
        functools.partial(_outproj_kernel, final=final),
        grid=(b, s // tm),
        in_specs=[row_spec(CONV_WIDTH), row_spec(RET_WIDTH),
                  pl.BlockSpec((1, ATT_HEADS, HEAD_DIM, tm), lambda i, j: (i, 0, 0, j)),
                  row_spec(d), row_spec(d), full2(w), full2(fg)],
        out_specs=row_spec(d),
        out_shape=jax.ShapeDtypeStruct((b, s, d), F32),
        compiler_params=pltpu.CompilerParams(
            dimension_semantics=("parallel", "parallel"), vmem_limit_bytes=VMEM_LIMIT),
        name="outproj",
    )(ya, yr, ym, sg, x, w, fg)


def _pack_weights(w_in_l):
    return w_in_l.astype(BF16), w_in_l[:, _C_MV:_C_MGATE].T.astype(BF16)


def _bf16_round(x):
    bits = np.asarray(x, np.float32).view(np.uint32).astype(np.uint64)
    bits = (bits + 0x7FFF + ((bits >> 16) & 1)) & 0xFFFF0000
    return bits.astype(np.uint32).view(np.float32)


def _q_bias_table():
    slopes = np.exp2(-8.0 * (np.arange(ATT_HEADS, dtype=np.float64) + 1.0) / ATT_HEADS)
    tab = np.zeros((ATT_HEADS, LANES), np.float32)
    for h in range(ATT_HEADS):
        base = HEAD_DIM if h % 2 == 0 else 0
        rest = slopes[h] * LOG2E
        for t in range(BIAS_TERMS):
            term = float(_bf16_round(rest))
            tab[h, base + 2 * t] = term
            tab[h, base + 2 * t + 1] = term
            rest -= term
    return jnp.asarray(tab)


def kernel(x, norm_g, w_in, conv_w, conv_b, conv_ln_g, conv_ln_b, conv_pw_w, conv_pw_b, w_out, final_g):
    b, s, d = x.shape
    tm = min(512, s)
    qtab = _q_bias_table()
    tables = _ret_tables()
    fg = final_g.reshape(1, d)
    for layer in range(DEPTH):
        w, wvt = _pack_weights(w_in[layer])
        conv_params = (conv_w[layer], conv_b[layer].reshape(1, -1), conv_ln_g[layer].reshape(1, -1),
                       conv_ln_b[layer].reshape(1, -1), conv_pw_w[layer].astype(BF16),
                       conv_pw_b[layer].reshape(1, -1))
        ya, yr, sg, qa, ka, vt = _front(x, norm_g[layer].reshape(1, d), w, wvt, qtab, conv_params, tables, tm=tm)
        ym = _moba(qa, ka, vt)
        x = _outproj(ya, yr, ym, sg, x, w_out[layer].astype(BF16), fg, tm=min(2 * tm, s),
                     final=(layer == DEPTH - 1))
    return x
```
